```python
import jax, jax.numpy as jnp
from jax import lax
import numpy as np

D_MODEL = 1024
BATCH = 16
SEQ = 4096
DEPTH = 1
DEC_BATCH = 32
DEC_SEQ = 16
PAST_LEN = 1024

CHUNK = 64
N_HEADS = 16
N_KV_HEADS = 4
HEAD_DIM = 64
Q_GROUP = N_HEADS // N_KV_HEADS
ATT_WIDTH = N_HEADS * HEAD_DIM
KV_WIDTH = N_KV_HEADS * HEAD_DIM
WINDOW = 128
WIN_CHUNKS = WINDOW // CHUNK
D_RNN = D_MODEL
N_LRU_BLOCKS = 8
LRU_BLOCK = D_RNN // N_LRU_BLOCKS
CONV_WIDTH = 4
LRU_C = 8.0
D_FF = 2816
EPS = 1e-6
NEG_INF = -1e30
ATT_SCALE = HEAD_DIM ** -0.5
SPLIT_POINTS = (D_RNN, 2 * D_RNN, 2 * D_RNN + ATT_WIDTH, 2 * D_RNN + ATT_WIDTH + KV_WIDTH,
                2 * D_RNN + ATT_WIDTH + 2 * KV_WIDTH, 2 * D_RNN + ATT_WIDTH + 2 * KV_WIDTH + D_MODEL)
D_IN = 2 * D_RNN + ATT_WIDTH + 2 * KV_WIDTH + 2 * D_MODEL

kernel_name = 'hawk_swa_sink_macaron_stream_step'


def rms_norm(x, g):
    x32 = x.astype(jnp.float32)
    y = x32 * lax.rsqrt(jnp.mean(x32 * x32, axis=-1, keepdims=True) + EPS)
    return (y * g.astype(jnp.float32)).astype(x.dtype)


def swiglu(x, w_gate, w_up, w_down):
    return (jax.nn.silu(x @ w_gate) * (x @ w_up)) @ w_down


def causal_conv(xin, prev, w, b):
    T = xin.shape[1]
    xp = jnp.concatenate([prev.astype(xin.dtype), xin], axis=1)
    out = b + sum(xp[:, j:j + T] * w[j] for j in range(CONV_WIDTH))
    return out, xp[:, -(CONV_WIDTH - 1):]


def rg_lru(xc, h0, w_rg, b_rg, w_ig, b_ig, lam):
    B, T, _ = xc.shape
    x32 = xc.astype(jnp.float32)
    xb = x32.reshape(B, T, N_LRU_BLOCKS, LRU_BLOCK)
    r = jax.nn.sigmoid(jnp.einsum('btnd,nde->btne', xb, w_rg.astype(jnp.float32)) + b_rg.astype(jnp.float32))
    i = jax.nn.sigmoid(jnp.einsum('btnd,nde->btne', xb, w_ig.astype(jnp.float32)) + b_ig.astype(jnp.float32))
    r = r.reshape(B, T, D_RNN)
    i = i.reshape(B, T, D_RNN)
    log_a = -LRU_C * r * jax.nn.softplus(-lam.astype(jnp.float32))
    a = jnp.exp(log_a)
    bterm = jnp.sqrt(-jnp.expm1(2.0 * log_a)) * (i * x32)
    bterm = bterm.at[:, 0].add(a[:, 0] * h0.astype(jnp.float32))

    def combine(left, right):
        a1, b1 = left
        a2, b2 = right
        return a1 * a2, a2 * b1 + b2

    _, h = lax.associative_scan(combine, (a, bterm), axis=1)
    return h.astype(xc.dtype), h[:, -1].astype(h0.dtype)


def sink_probs(scores, valid, sinks):
    s = jnp.where(valid, scores, NEG_INF)
    sink = sinks.astype(jnp.float32).reshape(N_KV_HEADS, Q_GROUP, 1, 1)
    m = jnp.maximum(jnp.max(s, axis=-1, keepdims=True), sink)
    p = jnp.exp(s - m)
    return p / (jnp.sum(p, axis=-1, keepdims=True) + jnp.exp(sink - m))


def banded_window_attention(q, k, v, sinks):
    B, T = q.shape[:2]
    nc = T // CHUNK
    qc = q.reshape(B, nc, CHUNK, N_KV_HEADS, Q_GROUP, HEAD_DIM)
    pad = ((0, 0), (WIN_CHUNKS * CHUNK, 0), (0, 0), (0, 0))
    kp = jnp.pad(k, pad).reshape(B, nc + WIN_CHUNKS, CHUNK, N_KV_HEADS, HEAD_DIM)
    vp = jnp.pad(v, pad).reshape(B, nc + WIN_CHUNKS, CHUNK, N_KV_HEADS, HEAD_DIM)
    kb = jnp.concatenate([kp[:, j:j + nc] for j in range(WIN_CHUNKS + 1)], axis=2)
    vb = jnp.concatenate([vp[:, j:j + nc] for j in range(WIN_CHUNKS + 1)], axis=2)
    key_chunk = (jnp.arange(nc)[:, None] - WIN_CHUNKS
                 + jnp.repeat(jnp.arange(WIN_CHUNKS + 1), CHUNK)[None, :])
    valid = (key_chunk >= 0)[:, None, None, None, :]
    scores = jnp.einsum('bcqkgd,bcskd->bckgqs', qc, kb, preferred_element_type=jnp.float32) * ATT_SCALE
    p = sink_probs(scores, valid, sinks)
    out = jnp.einsum('bckgqs,bcskd->bcqkgd', p.astype(v.dtype), vb)
    return out.reshape(B, T, ATT_WIDTH)


def cached_window_attention(q, k, v, cache_k, cache_v, sinks):
    B, S = q.shape[:2]
    cw = cache_k.shape[1]
    kk = jnp.concatenate([cache_k.astype(k.dtype), k], axis=1)
    vv = jnp.concatenate([cache_v.astype(v.dtype), v], axis=1)
    q_pos = PAST_LEN + jnp.arange(S)
    k_pos = jnp.concatenate([PAST_LEN - cw + jnp.arange(cw), q_pos])
    qch = (q_pos // CHUNK)[:, None]
    kch = (k_pos // CHUNK)[None, :]
    valid = (kch <= qch) & (qch - kch <= WIN_CHUNKS) & (k_pos[None, :] >= 0)
    qg = q.reshape(B, S, N_KV_HEADS, Q_GROUP, HEAD_DIM)
    scores = jnp.einsum('bqkgd,bskd->bkgqs', qg, kk, preferred_element_type=jnp.float32) * ATT_SCALE
    p = sink_probs(scores, valid, sinks)
    out = jnp.einsum('bkgqs,bskd->bqkgd', p.astype(vv.dtype), vv)
    return out.reshape(B, S, ATT_WIDTH)


def layer(x, conv_prev, h0, attend, p):
    B, T = x.shape[:2]
    h = x + 0.5 * swiglu(rms_norm(x, p['norm_ff1']), p['ff1_gate'], p['ff1_up'], p['ff1_down'])
    u = rms_norm(h, p['norm_mix'])
    x_rnn, g_rnn, q, k, v, gate_r, gate_a = jnp.split(u @ p['w_in'], SPLIT_POINTS, axis=-1)
    conv_out, conv_state = causal_conv(x_rnn, conv_prev, p['conv_w'], p['conv_b'])
    lru_out, h_last = rg_lru(conv_out, h0, p['w_rg'], p['b_rg'], p['w_ig'], p['b_ig'], p['lru_lambda'])
    rec = jax.nn.gelu(g_rnn) * lru_out
    q = q.reshape(B, T, N_HEADS, HEAD_DIM)
    k = k.reshape(B, T, N_KV_HEADS, HEAD_DIM)
    v = v.reshape(B, T, N_KV_HEADS, HEAD_DIM)
    att = attend(q, k, v, p['attn_sinks'])
    branch_r = rec @ p['w_branch'][:D_RNN]
    branch_a = att @ p['w_branch'][D_RNN:]
    merged = jax.nn.sigmoid(gate_r) * branch_r + jax.nn.sigmoid(gate_a) * branch_a
    h = h + merged @ p['w_out']
    h = h + 0.5 * swiglu(rms_norm(h, p['norm_ff2']), p['ff2_gate'], p['ff2_up'], p['ff2_down'])
    return h, k, v, conv_state, h_last


def setup_inputs(seed: int = 0) -> dict:
    key = jax.random.key(seed)
    ks = jax.random.split(key, 32)
    f32 = jnp.float32
    cw = min(WINDOW, PAST_LEN)

    def nrm(k, shape, scale):
        return jax.random.normal(k, shape, f32) * scale

    a0 = jax.random.uniform(ks[20], (DEPTH, D_RNN), f32, minval=0.9, maxval=0.999)
    return {
        'x_prompt': nrm(ks[0], (BATCH, SEQ, D_MODEL), 1.0),
        'x_sample': nrm(ks[1], (DEC_BATCH, DEC_SEQ, D_MODEL), 1.0),
        'cache_k': nrm(ks[2], (DEPTH, DEC_BATCH, cw, N_KV_HEADS, HEAD_DIM), 1.0),
        'cache_v': nrm(ks[3], (DEPTH, DEC_BATCH, cw, N_KV_HEADS, HEAD_DIM), 1.0),
        'state_conv': nrm(ks[4], (DEPTH, DEC_BATCH, CONV_WIDTH - 1, D_RNN), 1.0),
        'state_lru': nrm(ks[5], (DEPTH, DEC_BATCH, D_RNN), 0.5),
        'norm_ff1': 1.0 + nrm(ks[6], (DEPTH, D_MODEL), 0.01),
        'ff1_gate': nrm(ks[7], (DEPTH, D_MODEL, D_FF), D_MODEL ** -0.5),
        'ff1_up': nrm(ks[8], (DEPTH, D_MODEL, D_FF), D_MODEL ** -0.5),
        'ff1_down': nrm(ks[9], (DEPTH, D_FF, D_MODEL), D_FF ** -0.5),
        'norm_mix': 1.0 + nrm(ks[10], (DEPTH, D_MODEL), 0.01),
        'w_in': nrm(ks[11], (DEPTH, D_MODEL, D_IN), D_MODEL ** -0.5),
        'conv_w': nrm(ks[12], (DEPTH, CONV_WIDTH, D_RNN), CONV_WIDTH ** -0.5),
        'conv_b': nrm(ks[13], (DEPTH, D_RNN), 0.01),
        'w_rg': nrm(ks[14], (DEPTH, N_LRU_BLOCKS, LRU_BLOCK, LRU_BLOCK), LRU_BLOCK ** -0.5),
        'b_rg': nrm(ks[15], (DEPTH, N_LRU_BLOCKS, LRU_BLOCK), 0.01),
        'w_ig': nrm(ks[16], (DEPTH, N_LRU_BLOCKS, LRU_BLOCK, LRU_BLOCK), LRU_BLOCK ** -0.5),
        'b_ig': nrm(ks[17], (DEPTH, N_LRU_BLOCKS, LRU_BLOCK), 0.01),
        'lru_lambda': jnp.log(a0) - jnp.log1p(-a0),
        'attn_sinks': nrm(ks[18], (DEPTH, N_HEADS), 0.5),
        'w_branch': nrm(ks[19], (DEPTH, D_RNN + ATT_WIDTH, D_MODEL), D_RNN ** -0.5),
        'w_out': nrm(ks[21], (DEPTH, D_MODEL, D_MODEL), D_MODEL ** -0.5),
        'norm_ff2': 1.0 + nrm(ks[22], (DEPTH, D_MODEL), 0.01),
        'ff2_gate': nrm(ks[23], (DEPTH, D_MODEL, D_FF), D_MODEL ** -0.5),
        'ff2_up': nrm(ks[24], (DEPTH, D_MODEL, D_FF), D_MODEL ** -0.5),
        'ff2_down': nrm(ks[25], (DEPTH, D_FF, D_MODEL), D_FF ** -0.5),
        'norm_final': 1.0 + nrm(ks[26], (D_MODEL,), 0.01),
    }


def reference(x_prompt, x_sample, cache_k, cache_v, state_conv, state_lru,
              norm_ff1, ff1_gate, ff1_up, ff1_down, norm_mix, w_in, conv_w, conv_b,
              w_rg, b_rg, w_ig, b_ig, lru_lambda, attn_sinks, w_branch, w_out,
              norm_ff2, ff2_gate, ff2_up, ff2_down, norm_final):
    hp, hs = x_prompt, x_sample
    kp_l, vp_l, cp_l, lp_l, ks_l, vs_l, cs_l, ls_l = [], [], [], [], [], [], [], []
    for l in range(DEPTH):
        p = {'norm_ff1': norm_ff1[l], 'ff1_gate': ff1_gate[l], 'ff1_up': ff1_up[l], 'ff1_down': ff1_down[l],
             'norm_mix': norm_mix[l], 'w_in': w_in[l], 'conv_w': conv_w[l], 'conv_b': conv_b[l],
             'w_rg': w_rg[l], 'b_rg': b_rg[l], 'w_ig': w_ig[l], 'b_ig': b_ig[l],
             'lru_lambda': lru_lambda[l], 'attn_sinks': attn_sinks[l], 'w_branch': w_branch[l],
             'w_out': w_out[l], 'norm_ff2': norm_ff2[l], 'ff2_gate': ff2_gate[l], 'ff2_up': ff2_up[l],
             'ff2_down': ff2_down[l]}
        conv0 = jnp.zeros((hp.shape[0], CONV_WIDTH - 1, D_RNN), hp.dtype)
        h0 = jnp.zeros((hp.shape[0], D_RNN), state_lru.dtype)
        hp, k_p, v_p, conv_p, lru_p = layer(hp, conv0, h0, banded_window_attention, p)
        kp_l.append(k_p[:, -WINDOW:])
        vp_l.append(v_p[:, -WINDOW:])
        cp_l.append(conv_p)
        lp_l.append(lru_p)
        ck, cv = cache_k[l], cache_v[l]
        attend_s = lambda q, k, v, s, ck=ck, cv=cv: cached_window_attention(q, k, v, ck, cv, s)
        hs, k_s, v_s, conv_s, lru_s = layer(hs, state_conv[l], state_lru[l], attend_s, p)
        ks_l.append(k_s)
        vs_l.append(v_s)
        cs_l.append(conv_s)
        ls_l.append(lru_s)
    y_prompt = rms_norm(hp, norm_final)
    y_sample = rms_norm(hs, norm_final)
    return (y_prompt, y_sample,
            jnp.stack(kp_l), jnp.stack(vp_l), jnp.stack(cp_l), jnp.stack(lp_l),
            jnp.stack(ks_l), jnp.stack(vs_l), jnp.stack(cs_l), jnp.stack(ls_l))
```

```python
import functools

import numpy as np
import jax
import jax.numpy as jnp
from jax import lax
from jax.experimental import pallas as pl
from jax.experimental.pallas import tpu as pltpu

CHUNK = 64
N_HEADS = 16
N_KV_HEADS = 4
HEAD_DIM = 64
Q_GROUP = N_HEADS // N_KV_HEADS
WINDOW = 128
N_LRU_BLOCKS = 8
CONV_WIDTH = 4
LRU_C = 8.0
EPS = 1e-6
NEG_INF = -1e30
PAST_LEN = 1024
ATT_SCALE = HEAD_DIM ** -0.5

V7X_LANES = 128
V7X_SUBLANES = 8
V7X_MXU_DIM = 256
V7X_VMEM_BYTES = 64 * 1024 * 1024

F32 = jnp.float32
BF16 = jnp.bfloat16


def _dot(a, b):
    return jnp.dot(a, b, preferred_element_type=F32)


def _rms(x, w):
    return x * lax.rsqrt(jnp.mean(x * x, axis=-1, keepdims=True) + EPS) * w


def _const_spec(shape):
    zeros = (0,) * len(shape)
    return pl.BlockSpec(shape, lambda *_: zeros, pipeline_mode=pl.Buffered(1))


def _vmem_limit(block_bytes, resident_bytes, temp_bytes):
    need = 2 * block_bytes + resident_bytes + temp_bytes
    return int(min(V7X_VMEM_BYTES - (4 << 20), need))


def _ff_chunks(d_ff):
    step = 2 * V7X_MXU_DIM
    edges = list(range(0, d_ff, step)) + [d_ff]
    return tuple((a, b - a) for a, b in zip(edges[:-1], edges[1:]))


def _ff_body(x_ref, nw_ref, wg_ref, wu_ref, wd_ref, nf_ref, o_ref, *, chunks, final_norm):
    x = x_ref[...]
    xn = _rms(x, nw_ref[...]).astype(BF16)
    acc = None
    for c0, cs in chunks:
        gate = _dot(xn, wg_ref[:, c0:c0 + cs])
        up = _dot(xn, wu_ref[:, c0:c0 + cs])
        hid = (gate * jax.nn.sigmoid(gate) * up).astype(BF16)
        part = _dot(hid, wd_ref[c0:c0 + cs, :])
        acc = part if acc is None else acc + part
    y = x + 0.5 * acc
    if final_norm:
        y = _rms(y, nf_ref[...])
    o_ref[...] = y


def _ff(x, nw, wg, wu, wd, nf, *, final_norm, tm):
    rows, d = x.shape
    d_ff = wg.shape[1]
    assert rows % tm == 0
    chunks = _ff_chunks(d_ff)
    row_spec = pl.BlockSpec((tm, d), lambda i: (i, 0))
    widest = max(cs for _, cs in chunks)
    limit = _vmem_limit(
        block_bytes=2 * tm * d * 4,
        resident_bytes=3 * d * d_ff * 2 + 2 * 8 * d * 4,
        temp_bytes=tm * d * (2 + 4 + 4 + 4) + tm * widest * (4 + 4 + 4 + 2),
    )
    return pl.pallas_call(
        functools.partial(_ff_body, chunks=chunks, final_norm=final_norm),
        grid=(rows // tm,),
        in_specs=[row_spec, _const_spec((1, d)), _const_spec((d, d_ff)), _const_spec((d, d_ff)),
                  _const_spec((d_ff, d)), _const_spec((1, d))],
        out_specs=row_spec,
        out_shape=jax.ShapeDtypeStruct((rows, d), F32),
        compiler_params=pltpu.CompilerParams(dimension_semantics=("parallel",), vmem_limit_bytes=limit),
        name="ff",
    )(x, nw, wg, wu, wd, nf)


def _proj_body(h_ref, nw_ref, win_ref, xr_ref, g_ref, q_ref, k_ref, v_ref, gr_ref, ga_ref, *, d_rnn, att_w, kv_w, d):
    u = _rms(h_ref[...], nw_ref[...]).astype(BF16)
    c = 0
    xr_ref[...] = _dot(u, win_ref[:, c:c + d_rnn])
    c += d_rnn
    g_ref[...] = _dot(u, win_ref[:, c:c + d_rnn]).astype(BF16)
    c += d_rnn
    q_ref[...] = _dot(u, win_ref[:, c:c + att_w]).astype(BF16)
    c += att_w
    kv = _dot(u, win_ref[:, c:c + 2 * kv_w])
    k_ref[...] = kv[:, :kv_w]
    v_ref[...] = kv[:, kv_w:]
    c += 2 * kv_w
    gr_ref[...] = _dot(u, win_ref[:, c:c + d]).astype(BF16)
    c += d
    ga_ref[...] = _dot(u, win_ref[:, c:c + d]).astype(BF16)


def _proj(h, nw, win, *, d_rnn, att_w, kv_w, tm):
    rows, d = h.shape
    d_in = win.shape[1]
    assert rows % tm == 0 and d_in == 2 * d_rnn + att_w + 2 * kv_w + 2 * d

    def rs(w):
        return pl.BlockSpec((tm, w), lambda i: (i, 0))

    def sds(w, dt):
        return jax.ShapeDtypeStruct((rows, w), dt)

    out_bytes = tm * (d_rnn * 4 + d_rnn * 2 + att_w * 2 + 2 * kv_w * 4 + 2 * d * 2)
    limit = _vmem_limit(
        block_bytes=tm * d * 4 + out_bytes,
        resident_bytes=d * d_in * 2 + 8 * d * 4,
        temp_bytes=tm * d * (4 + 2) + 2 * tm * max(d_rnn, att_w, d) * 4,
    )
    return pl.pallas_call(
        functools.partial(_proj_body, d_rnn=d_rnn, att_w=att_w, kv_w=kv_w, d=d),
        grid=(rows // tm,),
        in_specs=[rs(d), _const_spec((1, d)), _const_spec((d, d_in))],
        out_specs=[rs(d_rnn), rs(d_rnn), rs(att_w), rs(kv_w), rs(kv_w), rs(d), rs(d)],
        out_shape=[sds(d_rnn, F32), sds(d_rnn, BF16), sds(att_w, BF16), sds(kv_w, F32), sds(kv_w, F32),
                   sds(d, BF16), sds(d, BF16)],
        compiler_params=pltpu.CompilerParams(dimension_semantics=("parallel",), vmem_limit_bytes=limit),
        name="proj",
    )(h, nw, win)


def _store_head_pairs(buf, row0, x):
    n = x.shape[0]
    lane = lax.broadcasted_iota(jnp.int32, (n, V7X_LANES), 1)
    for kv in range(N_KV_HEADS):
        blk = x[:, V7X_LANES * (kv // 2):V7X_LANES * (kv // 2 + 1)]
        swapped = pltpu.roll(blk, HEAD_DIM, 1)
        keep = (lane < HEAD_DIM) if kv % 2 == 0 else (lane >= HEAD_DIM)
        buf[kv, row0:row0 + n, :] = jnp.where(keep, blk, swapped).astype(BF16)


def _lru_scan(a, b, h0):
    rows = a.shape[0]
    sub = lax.broadcasted_iota(jnp.int32, (V7X_SUBLANES, a.shape[1]), 0)
    out = []
    carry = h0
    for r0 in range(0, rows, V7X_SUBLANES):
        ag = a[r0:r0 + V7X_SUBLANES]
        bg = b[r0:r0 + V7X_SUBLANES]
        shift = 1
        while shift < V7X_SUBLANES:
            take = sub >= shift
            a_prev = pltpu.roll(ag, shift, 0)
            b_prev = pltpu.roll(bg, shift, 0)
            bg = jnp.where(take, ag * b_prev + bg, bg)
            ag = jnp.where(take, ag * a_prev, ag)
            shift *= 2
        hg = ag * carry + bg
        out.append(hg)
        carry = hg[V7X_SUBLANES - 1:V7X_SUBLANES]
    return jnp.concatenate(out, axis=0)


def _mix_body(sinks_ref, xr_ref, g_ref, q_ref, k_ref, v_ref, gr_ref, ga_ref, h_ref,
              conv0_ref, lru0_ref, k0_ref, v0_ref,
              cw_ref, cb_ref, wgate_ref, bgate_ref, lam_ref, wbr_ref, wba_ref, wout_ref,
              h2_ref, lru_ref,
              xbuf, hstate, kbuf, vbuf, *, tt, qc, hist, mask_hist):
    t = pl.program_id(1)
    d_rnn = xr_ref.shape[1]
    pad = V7X_SUBLANES
    tail = CONV_WIDTH - 1

    @pl.when(t == 0)
    def _load_state():
        xbuf[pad - tail:pad, :] = conv0_ref[...]
        hstate[...] = lru0_ref[...]
        _store_head_pairs(kbuf, 0, k0_ref[...])
        _store_head_pairs(vbuf, 0, v0_ref[...])

    xr = xr_ref[...]
    xbuf[pad:pad + tt, :] = xr
    cw = cw_ref[...]
    xc = cb_ref[...] + cw[0:1] * xbuf[pad - 3:pad - 3 + tt, :]
    xc = xc + cw[1:2] * xbuf[pad - 2:pad - 2 + tt, :]
    xc = xc + cw[2:3] * xbuf[pad - 1:pad - 1 + tt, :]
    xc = xc + cw[3:4] * xr
    xbuf[pad - tail:pad, :] = xbuf[pad + tt - tail:pad + tt, :]

    xcb = xc.astype(BF16)
    blk = d_rnn // N_LRU_BLOCKS
    r_parts, i_parts = [], []
    for n in range(N_LRU_BLOCKS):
        z = _dot(xcb[:, n * blk:(n + 1) * blk], wgate_ref[n]) + bgate_ref[n]
        r_parts.append(z[:, :blk])
        i_parts.append(z[:, blk:])
    r = jax.nn.sigmoid(jnp.concatenate(r_parts, axis=1))
    i = jax.nn.sigmoid(jnp.concatenate(i_parts, axis=1))
    neg_lam = -lam_ref[...]
    softplus = jnp.maximum(neg_lam, 0.0) + jnp.log1p(jnp.exp(-jnp.abs(neg_lam)))
    log_a = -LRU_C * r * softplus
    a = jnp.exp(log_a)
    bterm = jnp.sqrt(-jnp.tanh(log_a) * (a * a + 1.0)) * (i * xc)
    hseq = _lru_scan(a, bterm, hstate[...])
    hstate[...] = hseq[tt - 1:tt]
    lru_ref[...] = hseq[tt - 1:tt]
    rec = (jax.nn.gelu(g_ref[...].astype(F32)) * hseq).astype(BF16)

    _store_head_pairs(kbuf, hist, k_ref[...])
    _store_head_pairs(vbuf, hist, v_ref[...])
    lane_row = lax.broadcasted_iota(jnp.int32, (1, V7X_LANES), 1)
    sel_lo = jnp.where(lane_row < HEAD_DIM, ATT_SCALE, 0.0).astype(BF16)
    sel_hi = jnp.where(lane_row >= HEAD_DIM, ATT_SCALE, 0.0).astype(BF16)
    lane_q = lax.broadcasted_iota(jnp.int32, (qc, V7X_LANES), 1)
    nkeys = hist + qc
    att_rows = []
    for j in range(tt // qc):
        q0 = j * qc
        heads = []
        for kv in range(N_KV_HEADS):
            c0 = kv * Q_GROUP * HEAD_DIM
            qa = q_ref[q0:q0 + qc, c0:c0 + V7X_LANES]
            qb = q_ref[q0:q0 + qc, c0 + V7X_LANES:c0 + 2 * V7X_LANES]
            lhs = jnp.concatenate([qa * sel_lo, qa * sel_hi, qb * sel_lo, qb * sel_hi], axis=0)
            keys = kbuf[kv, q0:q0 + nkeys, :]
            s = lax.dot_general(lhs, keys, (((1,), (1,)), ((), ())), preferred_element_type=F32)
            if mask_hist and q0 < hist:
                first_valid = jnp.where(t == 0, hist - q0, 0)
                col = lax.broadcasted_iota(jnp.int32, s.shape, 1)
                s = jnp.where(col >= first_valid, s, NEG_INF)
            sink = jnp.concatenate(
                [jnp.full((qc, 1), sinks_ref[kv * Q_GROUP + g], F32) for g in range(Q_GROUP)], axis=0)
            m = jnp.maximum(jnp.max(s, axis=-1, keepdims=True), sink)
            p = jnp.exp(s - m)
            denom = jnp.sum(p, axis=-1, keepdims=True) + jnp.exp(sink - m)
            o = _dot(p.astype(BF16), vbuf[kv, q0:q0 + nkeys, :]) / denom
            lo = lane_q < HEAD_DIM
            heads.append(jnp.where(lo, o[0:qc], o[qc:2 * qc]))
            heads.append(jnp.where(lo, o[2 * qc:3 * qc], o[3 * qc:4 * qc]))
        att_rows.append(jnp.concatenate(heads, axis=1))
    att = jnp.concatenate(att_rows, axis=0).astype(BF16)

    if tt >= hist:
        for kv in range(N_KV_HEADS):
            kbuf[kv, 0:hist, :] = kbuf[kv, tt:tt + hist, :]
            vbuf[kv, 0:hist, :] = vbuf[kv, tt:tt + hist, :]

    branch_r = _dot(rec, wbr_ref[...])
    branch_a = _dot(att, wba_ref[...])
    merged = (jax.nn.sigmoid(gr_ref[...].astype(F32)) * branch_r
              + jax.nn.sigmoid(ga_ref[...].astype(F32)) * branch_a)
    h2_ref[...] = h_ref[...] + _dot(merged.astype(BF16), wout_ref[...])


def _mix(sinks, xr, g, q, k, v, gr, ga, h, conv0, lru0, k0, v0, cw, cb, wgate, bgate, lam, wbr, wba, wout,
         *, tt, qc, mask_hist):
    bsz, seq, d = h.shape
    d_rnn = xr.shape[2]
    att_w = q.shape[2]
    kv_w = k.shape[2]
    hist = k0.shape[1]
    n_t = seq // tt
    assert seq % tt == 0 and tt % qc == 0 and qc % 16 == 0
    assert n_t == 1 or tt >= hist
    assert kv_w == N_KV_HEADS * HEAD_DIM and att_w == N_HEADS * HEAD_DIM and 2 * HEAD_DIM == V7X_LANES

    def ts(w):
        return pl.BlockSpec((None, tt, w), lambda b, t: (b, t, 0))

    def bs(r, w):
        return pl.BlockSpec((None, r, w), lambda b, t: (b, 0, 0))

    tile_bytes = tt * (d_rnn * 4 + d_rnn * 2 + att_w * 2 + 2 * kv_w * 4 + 2 * d * 2 + 2 * d * 4)
    limit = _vmem_limit(
        block_bytes=tile_bytes + 2 * hist * kv_w * 4,
        resident_bytes=(2 * d_rnn * 2 * (d_rnn // N_LRU_BLOCKS) + d_rnn * d + att_w * d + d * d) * 2
        + (8 + tt) * d_rnn * 4 + 2 * N_KV_HEADS * (hist + tt) * V7X_LANES * 2,
        temp_bytes=16 * tt * d_rnn * 4,
    )
    return pl.pallas_call(
        functools.partial(_mix_body, tt=tt, qc=qc, hist=hist, mask_hist=mask_hist),
        grid=(bsz, n_t),
        in_specs=[pl.BlockSpec(memory_space=pltpu.SMEM),
                  ts(d_rnn), ts(d_rnn), ts(att_w), ts(kv_w), ts(kv_w), ts(d), ts(d), ts(d),
                  bs(CONV_WIDTH - 1, d_rnn), bs(1, d_rnn), bs(hist, kv_w), bs(hist, kv_w),
                  _const_spec(cw.shape), _const_spec(cb.shape), _const_spec(wgate.shape),
                  _const_spec(bgate.shape), _const_spec(lam.shape), _const_spec(wbr.shape),
                  _const_spec(wba.shape), _const_spec(wout.shape)],
        out_specs=[ts(d), bs(1, d_rnn)],
        out_shape=[jax.ShapeDtypeStruct((bsz, seq, d), F32), jax.ShapeDtypeStruct((bsz, 1, d_rnn), F32)],
        scratch_shapes=[pltpu.VMEM((V7X_SUBLANES + tt, d_rnn), F32),
                        pltpu.VMEM((1, d_rnn), F32),
                        pltpu.VMEM((N_KV_HEADS, hist + tt, V7X_LANES), BF16),
                        pltpu.VMEM((N_KV_HEADS, hist + tt, V7X_LANES), BF16)],
        compiler_params=pltpu.CompilerParams(dimension_semantics=("parallel", "arbitrary"),
                                             vmem_limit_bytes=limit),
        name="mix",
    )(sinks, xr, g, q, k, v, gr, ga, h, conv0, lru0, k0, v0, cw, cb, wgate, bgate, lam, wbr, wba, wout)


def _layer(x, conv0, lru0, k0, v0, p, *, mask_hist, tm, tt, qc, final_norm):
    bsz, seq, d = x.shape
    rows = bsz * seq
    d_rnn = p["conv_w"].shape[1]
    att_w = N_HEADS * HEAD_DIM
    kv_w = N_KV_HEADS * HEAD_DIM
    h = _ff(x.reshape(rows, d), p["norm_ff1"], p["ff1_gate"], p["ff1_up"], p["ff1_down"], p["norm_final"],
            final_norm=False, tm=tm)
    xr, g, q, k, v, gr, ga = _proj(h, p["norm_mix"], p["w_in"], d_rnn=d_rnn, att_w=att_w, kv_w=kv_w, tm=tm)

    def b3(a):
        return a.reshape(bsz, seq, a.shape[1])

    h2, lru_last = _mix(p["attn_sinks"], b3(xr), b3(g), b3(q), b3(k), b3(v), b3(gr), b3(ga), b3(h),
                        conv0, lru0, k0, v0, p["conv_w"], p["conv_b"], p["w_gate"], p["b_gate"],
                        p["lru_lambda"], p["w_branch_r"], p["w_branch_a"], p["w_out"],
                        tt=tt, qc=qc, mask_hist=mask_hist)
    y = _ff(h2.reshape(rows, d), p["norm_ff2"], p["ff2_gate"], p["ff2_up"], p["ff2_down"], p["norm_final"],
            final_norm=final_norm, tm=tm)
    return y.reshape(bsz, seq, d), b3(k), b3(v), b3(xr), lru_last


def _sample_keys_all_visible(s, cw):
    q_pos = PAST_LEN + np.arange(s)
    k_pos = np.concatenate([PAST_LEN - cw + np.arange(cw), q_pos])
    qch = (q_pos // CHUNK)[:, None]
    kch = (k_pos // CHUNK)[None, :]
    valid = (kch <= qch) & (qch - kch <= WINDOW // CHUNK) & (k_pos[None, :] >= 0)
    return bool(valid.all())


def kernel(x_prompt, x_sample, cache_k, cache_v, state_conv, state_lru, norm_ff1, ff1_gate, ff1_up, ff1_down, norm_mix, w_in, conv_w, conv_b, w_rg, b_rg, w_ig, b_ig, lru_lambda, attn_sinks, w_branch, w_out, norm_ff2, ff2_gate, ff2_up, ff2_down, norm_final):
    depth = w_in.shape[0]
    bp, sp, d = x_prompt.shape
    bs_, ss, _ = x_sample.shape
    d_rnn = conv_w.shape[2]
    cw_len = cache_k.shape[2]
    kv_w = N_KV_HEADS * HEAD_DIM
    assert WINDOW == 2 * CHUNK and cw_len == WINDOW and sp % CHUNK == 0
    assert _sample_keys_all_visible(ss, cw_len)

    hp, hs = x_prompt, x_sample
    outs = [[] for _ in range(8)]
    for l in range(depth):
        p = {
            "norm_ff1": norm_ff1[l][None], "norm_mix": norm_mix[l][None], "norm_ff2": norm_ff2[l][None],
            "norm_final": norm_final[None],
            "ff1_gate": ff1_gate[l].astype(BF16), "ff1_up": ff1_up[l].astype(BF16),
            "ff1_down": ff1_down[l].astype(BF16),
            "ff2_gate": ff2_gate[l].astype(BF16), "ff2_up": ff2_up[l].astype(BF16),
            "ff2_down": ff2_down[l].astype(BF16),
            "w_in": w_in[l].astype(BF16),
            "conv_w": conv_w[l], "conv_b": conv_b[l][None],
            "w_gate": jnp.concatenate([w_rg[l], w_ig[l]], axis=-1).astype(BF16),
            "b_gate": jnp.concatenate([b_rg[l], b_ig[l]], axis=-1)[:, None, :],
            "lru_lambda": lru_lambda[l][None],
            "attn_sinks": attn_sinks[l],
            "w_branch_r": w_branch[l][:d_rnn].astype(BF16), "w_branch_a": w_branch[l][d_rnn:].astype(BF16),
            "w_out": w_out[l].astype(BF16),
        }
        last = l == depth - 1
        zeros = functools.partial(jnp.zeros, dtype=F32)
        hp, k_p, v_p, xr_p, lru_p = _layer(
            hp, zeros((bp, CONV_WIDTH - 1, d_rnn)), zeros((bp, 1, d_rnn)),
            zeros((bp, WINDOW, kv_w)), zeros((bp, WINDOW, kv_w)), p,
            mask_hist=True, tm=512, tt=256, qc=CHUNK, final_norm=last)
        hs, k_s, v_s, xr_s, lru_s = _layer(
            hs, state_conv[l], state_lru[l][:, None, :],
            cache_k[l].reshape(bs_, cw_len, kv_w), cache_v[l].reshape(bs_, cw_len, kv_w), p,
            mask_hist=False, tm=bs_ * ss, tt=ss, qc=ss, final_norm=last)
        tail = CONV_WIDTH - 1
        per_layer = (
            k_p[:, -WINDOW:].reshape(bp, WINDOW, N_KV_HEADS, HEAD_DIM),
            v_p[:, -WINDOW:].reshape(bp, WINDOW, N_KV_HEADS, HEAD_DIM),
            xr_p[:, -tail:], lru_p[:, 0],
            k_s.reshape(bs_, ss, N_KV_HEADS, HEAD_DIM), v_s.reshape(bs_, ss, N_KV_HEADS, HEAD_DIM),
            jnp.concatenate([state_conv[l], xr_s], axis=1)[:, -tail:], lru_s[:, 0],
        )
        for dst, val in zip(outs, per_layer):
            dst.append(val)
    return (hp, hs) + tuple(jnp.stack(o) for o in outs)
```

```python
import functools

import numpy as np
import jax
import jax.numpy as jnp
from jax import lax
from jax.experimental import pallas as pl
from jax.experimental.pallas import tpu as pltpu

CHUNK = 64
N_HEADS = 16
N_KV_HEADS = 4
HEAD_DIM = 64
Q_GROUP = N_HEADS // N_KV_HEADS
WINDOW = 128
N_LRU_BLOCKS = 8
CONV_WIDTH = 4
LRU_C = 8.0
EPS = 1e-6
NEG_INF = -1e30
PAST_LEN = 1024
ATT_SCALE = HEAD_DIM ** -0.5

V7X_LANES = 128
V7X_SUBLANES = 8
V7X_MXU_DIM = 256
V7X_VMEM_BYTES = 64 * 1024 * 1024

F32 = jnp.float32
BF16 = jnp.bfloat16


def _dot(a, b):
    return jnp.dot(a, b, preferred_element_type=F32)


def _rms(x, w):
    return x * lax.rsqrt(jnp.mean(x * x, axis=-1, keepdims=True) + EPS) * w


def _sigmoid(x):
    return 0.5 * jnp.tanh(0.5 * x) + 0.5


def _sqrt_nonneg(z):
    return jnp.exp(0.5 * jnp.log(z))


def _const_spec(shape):
    zeros = (0,) * len(shape)
    return pl.BlockSpec(shape, lambda *_: zeros, pipeline_mode=pl.Buffered(1))


def _vmem_limit(block_bytes, resident_bytes, temp_bytes):
    need = 2 * block_bytes + resident_bytes + temp_bytes
    return int(min(V7X_VMEM_BYTES - (4 << 20), need))


def _ff_chunks(d_ff):
    step = 2 * V7X_MXU_DIM
    edges = list(range(0, d_ff, step)) + [d_ff]
    return tuple((a, b - a) for a, b in zip(edges[:-1], edges[1:]))


def _ff_body(x_ref, nw_ref, wg_ref, wu_ref, wd_ref, nf_ref, o_ref, *, chunks, final_norm):
    x = x_ref[...]
    xn = _rms(x, nw_ref[...]).astype(BF16)
    acc = None
    for c0, cs in chunks:
        gate = _dot(xn, wg_ref[:, c0:c0 + cs])
        up = _dot(xn, wu_ref[:, c0:c0 + cs])
        hid = (gate * jax.nn.sigmoid(gate) * up).astype(BF16)
        part = _dot(hid, wd_ref[c0:c0 + cs, :])
        acc = part if acc is None else acc + part
    y = x + 0.5 * acc
    if final_norm:
        y = _rms(y, nf_ref[...])
    o_ref[...] = y


def _ff(x, nw, wg, wu, wd, nf, *, final_norm, tm):
    rows, d = x.shape
    d_ff = wg.shape[1]
    assert rows % tm == 0
    chunks = _ff_chunks(d_ff)
    row_spec = pl.BlockSpec((tm, d), lambda i: (i, 0))
    widest = max(cs for _, cs in chunks)
    limit = _vmem_limit(
        block_bytes=2 * tm * d * 4,
        resident_bytes=3 * d * d_ff * 2 + 2 * 8 * d * 4,
        temp_bytes=tm * d * (2 + 4 + 4 + 4) + tm * widest * (4 + 4 + 4 + 2),
    )
    return pl.pallas_call(
        functools.partial(_ff_body, chunks=chunks, final_norm=final_norm),
        grid=(rows // tm,),
        in_specs=[row_spec, _const_spec((1, d)), _const_spec((d, d_ff)), _const_spec((d, d_ff)),
                  _const_spec((d_ff, d)), _const_spec((1, d))],
        out_specs=row_spec,
        out_shape=jax.ShapeDtypeStruct((rows, d), F32),
        compiler_params=pltpu.CompilerParams(dimension_semantics=("parallel",), vmem_limit_bytes=limit),
        name="ff",
    )(x, nw, wg, wu, wd, nf)


def _store_head_pairs(buf, row0, x):
    n = x.shape[0]
    lane = lax.broadcasted_iota(jnp.int32, (n, V7X_LANES), 1)
    for kv in range(N_KV_HEADS):
        blk = x[:, V7X_LANES * (kv // 2):V7X_LANES * (kv // 2 + 1)]
        swapped = pltpu.roll(blk, HEAD_DIM, 1)
        keep = (lane < HEAD_DIM) if kv % 2 == 0 else (lane >= HEAD_DIM)
        buf[kv, row0:row0 + n, :] = jnp.where(keep, blk, swapped).astype(BF16)


def _causal_conv(x, tail8, cw, cb):
    sub8 = lax.broadcasted_iota(jnp.int32, tail8.shape, 0)

    def delayed(k):
        full = pltpu.roll(x, k, 0)
        head = jnp.where(sub8 < k, pltpu.roll(tail8, k, 0), full[0:V7X_SUBLANES])
        return jnp.concatenate([head, full[V7X_SUBLANES:]], axis=0)

    out = cb + cw[0:1] * delayed(3)
    out = out + cw[1:2] * delayed(2)
    out = out + cw[2:3] * delayed(1)
    return out + cw[3:4] * x


def _lru_scan(a, b, h0):
    rows = a.shape[0]
    sub = lax.broadcasted_iota(jnp.int32, (V7X_SUBLANES, a.shape[1]), 0)
    out = []
    carry = h0
    for r0 in range(0, rows, V7X_SUBLANES):
        ag = a[r0:r0 + V7X_SUBLANES]
        bg = b[r0:r0 + V7X_SUBLANES]
        shift = 1
        while shift < V7X_SUBLANES:
            take = sub >= shift
            a_prev = pltpu.roll(ag, shift, 0)
            b_prev = pltpu.roll(bg, shift, 0)
            bg = jnp.where(take, ag * b_prev + bg, bg)
            ag = jnp.where(take, ag * a_prev, ag)
            shift *= 2
        hg = ag * carry + bg
        out.append(hg)
        carry = hg[V7X_SUBLANES - 1:V7X_SUBLANES]
    return jnp.concatenate(out, axis=0)


def _attend_block(lhs, keys, vals, sink, first_valid):
    s = lax.dot_general(lhs, keys, (((1,), (1,)), ((), ())), preferred_element_type=F32)
    if first_valid is not None:
        col = lax.broadcasted_iota(jnp.int32, s.shape, 1)
        s = jnp.where(col >= first_valid, s, NEG_INF)
    m = jnp.maximum(jnp.max(s, axis=-1, keepdims=True), sink)
    p = jnp.exp(s - m)
    denom = jnp.sum(p, axis=-1, keepdims=True) + jnp.exp(sink - m)
    return _dot(p.astype(BF16), vals) / denom


def _mix_body(sinks_ref, h_ref, conv0_ref, lru0_ref, k0_ref, v0_ref,
              nw_ref, win_ref, cw_ref, cb_ref, wgate_ref, bgate_ref, lam_ref, wbr_ref, wba_ref, wout_ref,
              h2_ref, lru_ref, xtail_ref, knew_ref, vnew_ref,
              tailbuf, hstate, kbuf, vbuf, *, nb, tt, qc, hist, keep, mask_hist, d_rnn, att_w, kv_w):
    t = pl.program_id(1)
    d = h_ref.shape[-1]
    m = nb * tt

    @pl.when(t == 0)
    def _load_state():
        tailbuf[...] = jnp.zeros_like(tailbuf)
        tailbuf[:, V7X_SUBLANES - (CONV_WIDTH - 1):V7X_SUBLANES, :] = conv0_ref[...]
        hstate[...] = lru0_ref[...]
        for b in range(nb):
            _store_head_pairs(kbuf.at[b], 0, k0_ref[b])
            _store_head_pairs(vbuf.at[b], 0, v0_ref[b])

    hin = h_ref[...].reshape(m, d)
    u = _rms(hin, nw_ref[...]).astype(BF16)
    c_g = d_rnn
    c_q = c_g + d_rnn
    c_kv = c_q + att_w
    c_gr = c_kv + 2 * kv_w
    c_ga = c_gr + d
    xr = _dot(u, win_ref[:, 0:c_g])

    cw = cw_ref[...]
    cb = cb_ref[...]
    xc_rows = []
    for b in range(nb):
        xrb = xr[b * tt:(b + 1) * tt]
        xc_rows.append(_causal_conv(xrb, tailbuf[b], cw, cb))
        tailbuf[b] = xrb[tt - V7X_SUBLANES:tt]
        xtail_ref[b] = xrb[tt - V7X_SUBLANES:tt]
    xc = jnp.concatenate(xc_rows, axis=0)

    xcb = xc.astype(BF16)
    blk = d_rnn // N_LRU_BLOCKS
    r_parts, i_parts = [], []
    for n in range(N_LRU_BLOCKS):
        z = _dot(xcb[:, n * blk:(n + 1) * blk], wgate_ref[n]) + bgate_ref[n]
        r_parts.append(z[:, :blk])
        i_parts.append(z[:, blk:])
    neg_lam = -lam_ref[...]
    softplus = jnp.maximum(neg_lam, 0.0) + jnp.log1p(jnp.exp(-jnp.abs(neg_lam)))
    half_c = (0.5 * LRU_C) * softplus
    neg_log_a = half_c * jnp.tanh(0.5 * jnp.concatenate(r_parts, axis=1)) + half_c
    a = jnp.exp(-neg_log_a)
    i = _sigmoid(jnp.concatenate(i_parts, axis=1))
    bterm = _sqrt_nonneg(jnp.tanh(neg_log_a) * (a * a + 1.0)) * (i * xc)
    h_rows = []
    for b in range(nb):
        hb = _lru_scan(a[b * tt:(b + 1) * tt], bterm[b * tt:(b + 1) * tt], hstate[b])
        hstate[b] = hb[tt - 1:tt]
        lru_ref[b] = hb[tt - 1:tt]
        h_rows.append(hb)
    hseq = jnp.concatenate(h_rows, axis=0)
    rec = (jax.nn.gelu(_dot(u, win_ref[:, c_g:c_q])) * hseq).astype(BF16)

    q = (_dot(u, win_ref[:, c_q:c_kv]) * ATT_SCALE).astype(BF16)
    kv_new = _dot(u, win_ref[:, c_kv:c_gr])
    k_new = kv_new[:, :kv_w]
    v_new = kv_new[:, kv_w:]
    lane_row = lax.broadcasted_iota(jnp.int32, (1, V7X_LANES), 1)
    sel_lo = jnp.where(lane_row < HEAD_DIM, 1.0, 0.0).astype(BF16)
    sel_hi = jnp.where(lane_row >= HEAD_DIM, 1.0, 0.0).astype(BF16)
    lo = lax.broadcasted_iota(jnp.int32, (qc, V7X_LANES), 1) < HEAD_DIM
    sinks = [jnp.concatenate([jnp.full((qc, 1), sinks_ref[kv * Q_GROUP + g], F32) for g in range(Q_GROUP)],
                             axis=0) for kv in range(N_KV_HEADS)]
    nkeys = hist + qc
    att_rows = []
    for b in range(nb):
        kb = kbuf.at[b]
        vb = vbuf.at[b]
        _store_head_pairs(kb, hist, k_new[b * tt:(b + 1) * tt])
        _store_head_pairs(vb, hist, v_new[b * tt:(b + 1) * tt])
        knew_ref[b] = k_new[(b + 1) * tt - keep:(b + 1) * tt]
        vnew_ref[b] = v_new[(b + 1) * tt - keep:(b + 1) * tt]
        for j in range(tt // qc):
            q0 = j * qc
            first_valid = jnp.where(t == 0, hist - q0, 0) if (mask_hist and q0 < hist) else None
            heads = []
            for kv in range(N_KV_HEADS):
                c0 = kv * Q_GROUP * HEAD_DIM
                qa = q[b * tt + q0:b * tt + q0 + qc, c0:c0 + V7X_LANES]
                qb = q[b * tt + q0:b * tt + q0 + qc, c0 + V7X_LANES:c0 + 2 * V7X_LANES]
                lhs = jnp.concatenate([qa * sel_lo, qa * sel_hi, qb * sel_lo, qb * sel_hi], axis=0)
                o = _attend_block(lhs, kb[kv, q0:q0 + nkeys, :], vb[kv, q0:q0 + nkeys, :], sinks[kv],
                                  first_valid)
                heads.append(jnp.where(lo, o[0:qc], o[qc:2 * qc]))
                heads.append(jnp.where(lo, o[2 * qc:3 * qc], o[3 * qc:4 * qc]))
            att_rows.append(jnp.concatenate(heads, axis=1))
        if tt >= hist:
            for kv in range(N_KV_HEADS):
                kb[kv, 0:hist, :] = kb[kv, tt:tt + hist, :]
                vb[kv, 0:hist, :] = vb[kv, tt:tt + hist, :]
    att = jnp.concatenate(att_rows, axis=0).astype(BF16)

    branch_r = _dot(rec, wbr_ref[...])
    branch_a = _dot(att, wba_ref[...])
    merged = (_sigmoid(_dot(u, win_ref[:, c_gr:c_ga])) * branch_r
              + _sigmoid(_dot(u, win_ref[:, c_ga:c_ga + d])) * branch_a)
    h2_ref[...] = (hin + _dot(merged.astype(BF16), wout_ref[...])).reshape(nb, tt, d)


def _mix(sinks, h, conv0, lru0, k0, v0, nw, win, cw, cb, wgate, bgate, lam, wbr, wba, wout,
         *, nb, tt, qc, mask_hist):
    bsz, seq, d = h.shape
    d_rnn = cw.shape[1]
    att_w = N_HEADS * HEAD_DIM
    kv_w = N_KV_HEADS * HEAD_DIM
    hist = k0.shape[1]
    keep = min(tt, hist)
    n_t = seq // tt
    assert seq % tt == 0 and bsz % nb == 0 and tt % qc == 0 and qc % 16 == 0
    assert n_t == 1 or tt >= hist
    assert 2 * HEAD_DIM == V7X_LANES and win.shape[1] == 2 * d_rnn + att_w + 2 * kv_w + 2 * d

    def bs(r, w, follow_t=False):
        return pl.BlockSpec((nb, r, w), (lambda b, t: (b, t, 0)) if follow_t else (lambda b, t: (b, 0, 0)))

    m = nb * tt
    weight_bytes = (win.size + wgate.size + wbr.size + wba.size + wout.size) * 2
    limit = _vmem_limit(
        block_bytes=2 * m * d * 4 + nb * (8 * d_rnn + 4 * hist * kv_w) * 4,
        resident_bytes=weight_bytes + 2 * nb * N_KV_HEADS * (hist + tt) * V7X_LANES * 2 + nb * 16 * d_rnn * 4,
        temp_bytes=m * win.shape[1] * 4 + 12 * m * d_rnn * 4,
    )
    out_shape = [jax.ShapeDtypeStruct((bsz, seq, d), F32), jax.ShapeDtypeStruct((bsz, 1, d_rnn), F32),
                 jax.ShapeDtypeStruct((bsz, V7X_SUBLANES, d_rnn), F32),
                 jax.ShapeDtypeStruct((bsz, keep, kv_w), F32), jax.ShapeDtypeStruct((bsz, keep, kv_w), F32)]
    return pl.pallas_call(
        functools.partial(_mix_body, nb=nb, tt=tt, qc=qc, hist=hist, keep=keep, mask_hist=mask_hist,
                          d_rnn=d_rnn, att_w=att_w, kv_w=kv_w),
        grid=(bsz // nb, n_t),
        in_specs=[pl.BlockSpec(memory_space=pltpu.SMEM),
                  bs(tt, d, follow_t=True), bs(CONV_WIDTH - 1, d_rnn), bs(1, d_rnn), bs(hist, kv_w),
                  bs(hist, kv_w)]
        + [_const_spec(w.shape) for w in (nw, win, cw, cb, wgate, bgate, lam, wbr, wba, wout)],
        out_specs=[bs(tt, d, follow_t=True), bs(1, d_rnn), bs(V7X_SUBLANES, d_rnn), bs(keep, kv_w),
                   bs(keep, kv_w)],
        out_shape=out_shape,
        scratch_shapes=[pltpu.VMEM((nb, V7X_SUBLANES, d_rnn), F32),
                        pltpu.VMEM((nb, 1, d_rnn), F32),
                        pltpu.VMEM((nb, N_KV_HEADS, hist + tt, V7X_LANES), BF16),
                        pltpu.VMEM((nb, N_KV_HEADS, hist + tt, V7X_LANES), BF16)],
        compiler_params=pltpu.CompilerParams(dimension_semantics=("parallel", "arbitrary"),
                                             vmem_limit_bytes=limit),
        name="mix",
    )(sinks, h, conv0, lru0, k0, v0, nw, win, cw, cb, wgate, bgate, lam, wbr, wba, wout)


def _tiles(bsz, seq):
    if seq % 256 == 0:
        return 512, 1, 256, CHUNK
    assert seq <= CHUNK and (bsz * seq) % V7X_SUBLANES == 0
    return bsz * seq, bsz, seq, seq


def _layer(x, conv0, lru0, k0, v0, p, *, mask_hist, final_norm):
    bsz, seq, d = x.shape
    rows = bsz * seq
    tm, nb, tt, qc = _tiles(bsz, seq)
    h = _ff(x.reshape(rows, d), p["norm_ff1"], p["ff1_gate"], p["ff1_up"], p["ff1_down"], p["norm_final"],
            final_norm=False, tm=tm)
    h2, lru_last, x_tail, k_new, v_new = _mix(
        p["attn_sinks"], h.reshape(bsz, seq, d), conv0, lru0, k0, v0, p["norm_mix"], p["w_in"],
        p["conv_w"], p["conv_b"], p["w_gate"], p["b_gate"], p["lru_lambda"], p["w_branch_r"],
        p["w_branch_a"], p["w_out"], nb=nb, tt=tt, qc=qc, mask_hist=mask_hist)
    y = _ff(h2.reshape(rows, d), p["norm_ff2"], p["ff2_gate"], p["ff2_up"], p["ff2_down"], p["norm_final"],
            final_norm=final_norm, tm=tm)
    return y.reshape(bsz, seq, d), k_new, v_new, x_tail[:, -(CONV_WIDTH - 1):], lru_last[:, 0]


def _sample_keys_all_visible(s, cw):
    q_pos = PAST_LEN + np.arange(s)
    k_pos = np.concatenate([PAST_LEN - cw + np.arange(cw), q_pos])
    qch = (q_pos // CHUNK)[:, None]
    kch = (k_pos // CHUNK)[None, :]
    valid = (kch <= qch) & (qch - kch <= WINDOW // CHUNK) & (k_pos[None, :] >= 0)
    return bool(valid.all())


def kernel(x_prompt, x_sample, cache_k, cache_v, state_conv, state_lru, norm_ff1, ff1_gate, ff1_up, ff1_down, norm_mix, w_in, conv_w, conv_b, w_rg, b_rg, w_ig, b_ig, lru_lambda, attn_sinks, w_branch, w_out, norm_ff2, ff2_gate, ff2_up, ff2_down, norm_final):
    depth = w_in.shape[0]
    bp, sp, d = x_prompt.shape
    bs_, ss, _ = x_sample.shape
    d_rnn = conv_w.shape[2]
    cw_len = cache_k.shape[2]
    kv_w = N_KV_HEADS * HEAD_DIM
    assert WINDOW == 2 * CHUNK and cw_len == WINDOW and sp % CHUNK == 0
    assert _sample_keys_all_visible(ss, cw_len)

    hp, hs = x_prompt, x_sample
    outs = [[] for _ in range(8)]
    for l in range(depth):
        p = {
            "norm_ff1": norm_ff1[l][None], "norm_mix": norm_mix[l][None], "norm_ff2": norm_ff2[l][None],
            "norm_final": norm_final[None],
            "ff1_gate": ff1_gate[l].astype(BF16), "ff1_up": ff1_up[l].astype(BF16),
            "ff1_down": ff1_down[l].astype(BF16),
            "ff2_gate": ff2_gate[l].astype(BF16), "ff2_up": ff2_up[l].astype(BF16),
            "ff2_down": ff2_down[l].astype(BF16),
            "w_in": w_in[l].astype(BF16),
            "conv_w": conv_w[l], "conv_b": conv_b[l][None],
            "w_gate": jnp.concatenate([w_rg[l], w_ig[l]], axis=-1).astype(BF16),
            "b_gate": jnp.concatenate([b_rg[l], b_ig[l]], axis=-1)[:, None, :],
            "lru_lambda": lru_lambda[l][None],
            "attn_sinks": attn_sinks[l],
            "w_branch_r": w_branch[l][:d_rnn].astype(BF16), "w_branch_a": w_branch[l][d_rnn:].astype(BF16),
            "w_out": w_out[l].astype(BF16),
        }
        last = l == depth - 1
        zeros = functools.partial(jnp.zeros, dtype=F32)
        hp, k_p, v_p, conv_p, lru_p = _layer(
            hp, zeros((bp, CONV_WIDTH - 1, d_rnn)), zeros((bp, 1, d_rnn)),
            zeros((bp, WINDOW, kv_w)), zeros((bp, WINDOW, kv_w)), p, mask_hist=True, final_norm=last)
        hs, k_s, v_s, conv_s, lru_s = _layer(
            hs, state_conv[l], state_lru[l][:, None, :],
            cache_k[l].reshape(bs_, cw_len, kv_w), cache_v[l].reshape(bs_, cw_len, kv_w), p,
            mask_hist=False, final_norm=last)
        per_layer = (
            k_p.reshape(bp, WINDOW, N_KV_HEADS, HEAD_DIM), v_p.reshape(bp, WINDOW, N_KV_HEADS, HEAD_DIM),
            conv_p, lru_p,
            k_s.reshape(bs_, ss, N_KV_HEADS, HEAD_DIM), v_s.reshape(bs_, ss, N_KV_HEADS, HEAD_DIM),
            conv_s, lru_s,
        )
        for dst, val in zip(outs, per_layer):
            dst.append(val)
    return (hp, hs) + tuple(jnp.stack(o) for o in outs)
```

```python
import functools

import numpy as np
import jax
import jax.numpy as jnp
from jax import lax
from jax.experimental import pallas as pl
from jax.experimental.pallas import tpu as pltpu

CHUNK = 64
N_HEADS = 16
N_KV_HEADS = 4
HEAD_DIM = 64
Q_GROUP = N_HEADS // N_KV_HEADS
WINDOW = 128
N_LRU_BLOCKS = 8
CONV_WIDTH = 4
LRU_C = 8.0
EPS = 1e-6
NEG_INF = -1e30
PAST_LEN = 1024
ATT_SCALE = HEAD_DIM ** -0.5

V7X_LANES = 128
V7X_SUBLANES = 8
V7X_MXU_DIM = 256
V7X_VMEM_BYTES = 64 * 1024 * 1024

F32 = jnp.float32
BF16 = jnp.bfloat16


def _dot(a, b):
    return jnp.dot(a, b, preferred_element_type=F32)


def _rms(x, w):
    return x * lax.rsqrt(jnp.mean(x * x, axis=-1, keepdims=True) + EPS) * w


def _sigmoid(x):
    return 0.5 * jnp.tanh(0.5 * x) + 0.5


def _sqrt_nonneg(z):
    return jnp.exp(0.5 * jnp.log(z))


def _const_spec(shape):
    zeros = (0,) * len(shape)
    return pl.BlockSpec(shape, lambda *_: zeros, pipeline_mode=pl.Buffered(1))


def _vmem_limit(block_bytes, resident_bytes, temp_bytes):
    need = 2 * block_bytes + resident_bytes + temp_bytes
    return int(min(V7X_VMEM_BYTES - (4 << 20), need))


def _ff_chunks(d_ff):
    step = 2 * V7X_MXU_DIM
    edges = list(range(0, d_ff, step)) + [d_ff]
    return tuple((a, b - a) for a, b in zip(edges[:-1], edges[1:]))


def _ff_body(x_ref, nw_ref, wg_ref, wu_ref, wd_ref, nf_ref, o_ref, *, chunks, final_norm):
    x = x_ref[...]
    xn = _rms(x, nw_ref[...]).astype(BF16)
    acc = None
    for c0, cs in chunks:
        gate = _dot(xn, wg_ref[:, c0:c0 + cs])
        up = _dot(xn, wu_ref[:, c0:c0 + cs])
        hid = (gate * jax.nn.sigmoid(gate) * up).astype(BF16)
        part = _dot(hid, wd_ref[c0:c0 + cs, :])
        acc = part if acc is None else acc + part
    y = x + 0.5 * acc
    if final_norm:
        y = _rms(y, nf_ref[...])
    o_ref[...] = y


def _ff(x, nw, wg, wu, wd, nf, *, final_norm, tm):
    rows, d = x.shape
    d_ff = wg.shape[1]
    assert rows % tm == 0
    chunks = _ff_chunks(d_ff)
    row_spec = pl.BlockSpec((tm, d), lambda i: (i, 0))
    widest = max(cs for _, cs in chunks)
    limit = _vmem_limit(
        block_bytes=2 * tm * d * 4,
        resident_bytes=3 * d * d_ff * 2 + 2 * 8 * d * 4,
        temp_bytes=tm * d * (2 + 4 + 4 + 4) + tm * widest * (4 + 4 + 4 + 2),
    )
    return pl.pallas_call(
        functools.partial(_ff_body, chunks=chunks, final_norm=final_norm),
        grid=(rows // tm,),
        in_specs=[row_spec, _const_spec((1, d)), _const_spec((d, d_ff)), _const_spec((d, d_ff)),
                  _const_spec((d_ff, d)), _const_spec((1, d))],
        out_specs=row_spec,
        out_shape=jax.ShapeDtypeStruct((rows, d), F32),
        compiler_params=pltpu.CompilerParams(dimension_semantics=("parallel",), vmem_limit_bytes=limit),
        name="ff",
    )(x, nw, wg, wu, wd, nf)


def _store_head_pairs(buf, row0, x):
    n = x.shape[0]
    lane = lax.broadcasted_iota(jnp.int32, (n, V7X_LANES), 1)
    for kv in range(N_KV_HEADS):
        blk = x[:, V7X_LANES * (kv // 2):V7X_LANES * (kv // 2 + 1)]
        swapped = pltpu.roll(blk, HEAD_DIM, 1)
        keep = (lane < HEAD_DIM) if kv % 2 == 0 else (lane >= HEAD_DIM)
        buf[kv, row0:row0 + n, :] = jnp.where(keep, blk, swapped).astype(BF16)


def _causal_conv(x, tail8, cw, cb):
    sub8 = lax.broadcasted_iota(jnp.int32, tail8.shape, 0)

    def delayed(k):
        full = pltpu.roll(x, k, 0)
        head = jnp.where(sub8 < k, pltpu.roll(tail8, k, 0), full[0:V7X_SUBLANES])
        return jnp.concatenate([head, full[V7X_SUBLANES:]], axis=0)

    out = cb + cw[0:1] * delayed(3)
    out = out + cw[1:2] * delayed(2)
    out = out + cw[2:3] * delayed(1)
    return out + cw[3:4] * x


def _lru_scan(a, b, h0):
    rows = a.shape[0]
    sub = lax.broadcasted_iota(jnp.int32, (V7X_SUBLANES, a.shape[1]), 0)
    out = []
    carry = h0
    for r0 in range(0, rows, V7X_SUBLANES):
        ag = a[r0:r0 + V7X_SUBLANES]
        bg = b[r0:r0 + V7X_SUBLANES]
        shift = 1
        while shift < V7X_SUBLANES:
            take = sub >= shift
            a_prev = pltpu.roll(ag, shift, 0)
            b_prev = pltpu.roll(bg, shift, 0)
            bg = jnp.where(take, ag * b_prev + bg, bg)
            ag = jnp.where(take, ag * a_prev, ag)
            shift *= 2
        hg = ag * carry + bg
        out.append(hg)
        carry = hg[V7X_SUBLANES - 1:V7X_SUBLANES]
    return jnp.concatenate(out, axis=0)


def _attend_group(lhs, keys, vals, sinks, first_valid, filler):
    nt = (((1,), (1,)), ((), ()))
    tn = (((0,), (0,)), ((), ()))
    s = [lax.dot_general(k, q, nt, preferred_element_type=F32) for k, q in zip(keys, lhs)]
    if filler is not None:
        filler()
    if first_valid is not None:
        row = lax.broadcasted_iota(jnp.int32, s[0].shape, 0)
        s = [jnp.where(row >= first_valid, x, NEG_INF) for x in s]
    m = [jnp.maximum(jnp.max(x, axis=0, keepdims=True), sk) for x, sk in zip(s, sinks)]
    p = [jnp.exp(x - mx) for x, mx in zip(s, m)]
    denom = [jnp.sum(x, axis=0, keepdims=True) + jnp.exp(sk - mx) for x, sk, mx in zip(p, sinks, m)]
    pn = [(x * (1.0 / dn)).astype(BF16) for x, dn in zip(p, denom)]
    out = [lax.dot_general(x, v, tn, preferred_element_type=F32) for x, v in zip(pn, vals)]
    return out


def _conv_and_gate_logits(xr, cw_ref, cb_ref, wgate_ref, bgate_ref, tailbuf, xtail_ref, *, nb, tt, d_rnn):
    cw = cw_ref[...]
    cb = cb_ref[...]
    xc_rows = []
    for b in range(nb):
        xrb = xr[b * tt:(b + 1) * tt]
        xc_rows.append(_causal_conv(xrb, tailbuf[b], cw, cb))
        tailbuf[b] = xrb[tt - V7X_SUBLANES:tt]
        xtail_ref[b] = xrb[tt - V7X_SUBLANES:tt]
    xc = jnp.concatenate(xc_rows, axis=0)
    xcb = xc.astype(BF16)
    blk = d_rnn // N_LRU_BLOCKS
    r_parts, i_parts = [], []
    for n in range(N_LRU_BLOCKS):
        z = _dot(xcb[:, n * blk:(n + 1) * blk], wgate_ref[n]) + bgate_ref[n]
        r_parts.append(z[:, :blk])
        i_parts.append(z[:, blk:])
    return xc, jnp.concatenate(r_parts, axis=1), jnp.concatenate(i_parts, axis=1)


def _lru_coefficients(xc, r_pre, i_pre, lam_ref):
    neg_lam = -lam_ref[...]
    softplus = jnp.maximum(neg_lam, 0.0) + jnp.log1p(jnp.exp(-jnp.abs(neg_lam)))
    half_c = (0.5 * LRU_C) * softplus
    neg_log_a = half_c * jnp.tanh(0.5 * r_pre) + half_c
    a = jnp.exp(-neg_log_a)
    return a, _sqrt_nonneg(jnp.tanh(neg_log_a) * (a * a + 1.0)) * (_sigmoid(i_pre) * xc)


def _lru_states(a, bterm, hstate, lru_ref, *, nb, tt):
    h_rows = []
    for b in range(nb):
        hb = _lru_scan(a[b * tt:(b + 1) * tt], bterm[b * tt:(b + 1) * tt], hstate[b])
        hstate[b] = hb[tt - 1:tt]
        lru_ref[b] = hb[tt - 1:tt]
        h_rows.append(hb)
    return jnp.concatenate(h_rows, axis=0)


def _attention(u, win_ref, sinks_ref, kbuf, vbuf, knew_ref, vnew_ref, t, fillers, *, nb, tt, qc, hist, keep,
               mask_hist, c_q, c_kv, c_gr, kv_w):
    q = (_dot(u, win_ref[:, c_q:c_kv]) * ATT_SCALE).astype(BF16)
    kv_new = _dot(u, win_ref[:, c_kv:c_gr])
    k_new = kv_new[:, :kv_w]
    v_new = kv_new[:, kv_w:]
    lane_row = lax.broadcasted_iota(jnp.int32, (1, V7X_LANES), 1)
    sel_lo = jnp.where(lane_row < HEAD_DIM, 1.0, 0.0).astype(BF16)
    sel_hi = jnp.where(lane_row >= HEAD_DIM, 1.0, 0.0).astype(BF16)
    lo = lax.broadcasted_iota(jnp.int32, (qc, V7X_LANES), 1) < HEAD_DIM
    group_of_lane = lax.broadcasted_iota(jnp.int32, (1, Q_GROUP * qc), 1) // qc
    sinks = []
    for kv in range(N_KV_HEADS):
        row = jnp.full((1, Q_GROUP * qc), sinks_ref[kv * Q_GROUP], F32)
        for g in range(1, Q_GROUP):
            row = jnp.where(group_of_lane == g, sinks_ref[kv * Q_GROUP + g], row)
        sinks.append(row)
    nkeys = hist + qc
    fillers = list(fillers)
    att_rows = []
    for b in range(nb):
        kb = kbuf.at[b]
        vb = vbuf.at[b]
        _store_head_pairs(kb, hist, k_new[b * tt:(b + 1) * tt])
        _store_head_pairs(vb, hist, v_new[b * tt:(b + 1) * tt])
        knew_ref[b] = k_new[(b + 1) * tt - keep:(b + 1) * tt]
        vnew_ref[b] = v_new[(b + 1) * tt - keep:(b + 1) * tt]
        for j in range(tt // qc):
            q0 = j * qc
            first_valid = jnp.where(t == 0, hist - q0, 0) if (mask_hist and q0 < hist) else None
            lhs = []
            for kv in range(N_KV_HEADS):
                c0 = kv * Q_GROUP * HEAD_DIM
                qa = q[b * tt + q0:b * tt + q0 + qc, c0:c0 + V7X_LANES]
                qb = q[b * tt + q0:b * tt + q0 + qc, c0 + V7X_LANES:c0 + 2 * V7X_LANES]
                lhs.append(jnp.concatenate([qa * sel_lo, qa * sel_hi, qb * sel_lo, qb * sel_hi], axis=0))
            outs = _attend_group(
                lhs, [kb[kv, q0:q0 + nkeys, :] for kv in range(N_KV_HEADS)],
                [vb[kv, q0:q0 + nkeys, :] for kv in range(N_KV_HEADS)], sinks, first_valid,
                fillers.pop(0) if fillers else None)
            heads = []
            for o in outs:
                heads.append(jnp.where(lo, o[0:qc], o[qc:2 * qc]))
                heads.append(jnp.where(lo, o[2 * qc:3 * qc], o[3 * qc:4 * qc]))
            att_rows.append(jnp.concatenate(heads, axis=1))
        if tt >= hist:
            for kv in range(N_KV_HEADS):
                kb[kv, 0:hist, :] = kb[kv, tt:tt + hist, :]
                vb[kv, 0:hist, :] = vb[kv, tt:tt + hist, :]
    for f in fillers:
        f()
    return jnp.concatenate(att_rows, axis=0).astype(BF16)


def _mix_body(sinks_ref, h_ref, conv0_ref, lru0_ref, k0_ref, v0_ref,
              nw_ref, win_ref, cw_ref, cb_ref, wgate_ref, bgate_ref, lam_ref, wbr_ref, wba_ref, wout_ref,
              h2_ref, lru_ref, xtail_ref, knew_ref, vnew_ref,
              tailbuf, hstate, kbuf, vbuf, *, nb, tt, qc, hist, keep, mask_hist, d_rnn, att_w, kv_w):
    t = pl.program_id(1)
    d = h_ref.shape[-1]
    m = nb * tt

    @pl.when(t == 0)
    def _load_state():
        tailbuf[...] = jnp.zeros_like(tailbuf)
        tailbuf[:, V7X_SUBLANES - (CONV_WIDTH - 1):V7X_SUBLANES, :] = conv0_ref[...]
        hstate[...] = lru0_ref[...]
        for b in range(nb):
            _store_head_pairs(kbuf.at[b], 0, k0_ref[b])
            _store_head_pairs(vbuf.at[b], 0, v0_ref[b])

    hin = h_ref[...].reshape(m, d)
    u = _rms(hin, nw_ref[...]).astype(BF16)
    c_g = d_rnn
    c_q = c_g + d_rnn
    c_kv = c_q + att_w
    c_gr = c_kv + 2 * kv_w
    c_ga = c_gr + d
    xr = _dot(u, win_ref[:, 0:c_g])
    st = {}

    def stage_conv():
        st["xc"], st["r_pre"], st["i_pre"] = _conv_and_gate_logits(
            xr, cw_ref, cb_ref, wgate_ref, bgate_ref, tailbuf, xtail_ref, nb=nb, tt=tt, d_rnn=d_rnn)
        st["g_rnn"] = _dot(u, win_ref[:, c_g:c_q])

    def stage_coefficients():
        st["a"], st["b"] = _lru_coefficients(st["xc"], st["r_pre"], st["i_pre"], lam_ref)
        st["gate_r"] = _sigmoid(_dot(u, win_ref[:, c_gr:c_ga]))

    def stage_scan():
        st["h"] = _lru_states(st["a"], st["b"], hstate, lru_ref, nb=nb, tt=tt)
        st["gate_a"] = _sigmoid(_dot(u, win_ref[:, c_ga:c_ga + d]))

    def stage_branch():
        rec = (jax.nn.gelu(st["g_rnn"]) * st["h"]).astype(BF16)
        st["gated_r"] = st["gate_r"] * _dot(rec, wbr_ref[...])

    att = _attention(u, win_ref, sinks_ref, kbuf, vbuf, knew_ref, vnew_ref, t,
                     [stage_conv, stage_coefficients, stage_scan, stage_branch], nb=nb, tt=tt, qc=qc,
                     hist=hist, keep=keep, mask_hist=mask_hist, c_q=c_q, c_kv=c_kv, c_gr=c_gr, kv_w=kv_w)
    merged = st["gated_r"] + st["gate_a"] * _dot(att, wba_ref[...])
    h2_ref[...] = (hin + _dot(merged.astype(BF16), wout_ref[...])).reshape(nb, tt, d)


def _mix(sinks, h, conv0, lru0, k0, v0, nw, win, cw, cb, wgate, bgate, lam, wbr, wba, wout,
         *, nb, tt, qc, mask_hist):
    bsz, seq, d = h.shape
    d_rnn = cw.shape[1]
    att_w = N_HEADS * HEAD_DIM
    kv_w = N_KV_HEADS * HEAD_DIM
    hist = k0.shape[1]
    keep = min(tt, hist)
    n_t = seq // tt
    assert seq % tt == 0 and bsz % nb == 0 and tt % qc == 0 and qc % 16 == 0
    assert n_t == 1 or tt >= hist
    assert 2 * HEAD_DIM == V7X_LANES and win.shape[1] == 2 * d_rnn + att_w + 2 * kv_w + 2 * d

    def bs(r, w, follow_t=False):
        return pl.BlockSpec((nb, r, w), (lambda b, t: (b, t, 0)) if follow_t else (lambda b, t: (b, 0, 0)))

    m = nb * tt
    weight_bytes = (win.size + wgate.size + wbr.size + wba.size + wout.size) * 2
    limit = _vmem_limit(
        block_bytes=2 * m * d * 4 + nb * (8 * d_rnn + 4 * hist * kv_w) * 4,
        resident_bytes=weight_bytes + 2 * nb * N_KV_HEADS * (hist + tt) * V7X_LANES * 2 + nb * 16 * d_rnn * 4,
        temp_bytes=m * win.shape[1] * 4 + 12 * m * d_rnn * 4,
    )
    out_shape = [jax.ShapeDtypeStruct((bsz, seq, d), F32), jax.ShapeDtypeStruct((bsz, 1, d_rnn), F32),
                 jax.ShapeDtypeStruct((bsz, V7X_SUBLANES, d_rnn), F32),
                 jax.ShapeDtypeStruct((bsz, keep, kv_w), F32), jax.ShapeDtypeStruct((bsz, keep, kv_w), F32)]
    return pl.pallas_call(
        functools.partial(_mix_body, nb=nb, tt=tt, qc=qc, hist=hist, keep=keep, mask_hist=mask_hist,
                          d_rnn=d_rnn, att_w=att_w, kv_w=kv_w),
        grid=(bsz // nb, n_t),
        in_specs=[pl.BlockSpec(memory_space=pltpu.SMEM),
                  bs(tt, d, follow_t=True), bs(CONV_WIDTH - 1, d_rnn), bs(1, d_rnn), bs(hist, kv_w),
                  bs(hist, kv_w)]
        + [_const_spec(w.shape) for w in (nw, win, cw, cb, wgate, bgate, lam, wbr, wba, wout)],
        out_specs=[bs(tt, d, follow_t=True), bs(1, d_rnn), bs(V7X_SUBLANES, d_rnn), bs(keep, kv_w),
                   bs(keep, kv_w)],
        out_shape=out_shape,
        scratch_shapes=[pltpu.VMEM((nb, V7X_SUBLANES, d_rnn), F32),
                        pltpu.VMEM((nb, 1, d_rnn), F32),
                        pltpu.VMEM((nb, N_KV_HEADS, hist + tt, V7X_LANES), BF16),
                        pltpu.VMEM((nb, N_KV_HEADS, hist + tt, V7X_LANES), BF16)],
        compiler_params=pltpu.CompilerParams(dimension_semantics=("parallel", "arbitrary"),
                                             vmem_limit_bytes=limit),
        name="mix",
    )(sinks, h, conv0, lru0, k0, v0, nw, win, cw, cb, wgate, bgate, lam, wbr, wba, wout)


def _tiles(bsz, seq):
    if seq % 256 == 0:
        return 512, 1, 256, CHUNK
    assert seq <= CHUNK and (bsz * seq) % V7X_SUBLANES == 0
    return bsz * seq, bsz, seq, seq


def _layer(x, conv0, lru0, k0, v0, p, *, mask_hist, final_norm):
    bsz, seq, d = x.shape
    rows = bsz * seq
    tm, nb, tt, qc = _tiles(bsz, seq)
    h = _ff(x.reshape(rows, d), p["norm_ff1"], p["ff1_gate"], p["ff1_up"], p["ff1_down"], p["norm_final"],
            final_norm=False, tm=tm)
    h2, lru_last, x_tail, k_new, v_new = _mix(
        p["attn_sinks"], h.reshape(bsz, seq, d), conv0, lru0, k0, v0, p["norm_mix"], p["w_in"],
        p["conv_w"], p["conv_b"], p["w_gate"], p["b_gate"], p["lru_lambda"], p["w_branch_r"],
        p["w_branch_a"], p["w_out"], nb=nb, tt=tt, qc=qc, mask_hist=mask_hist)
    y = _ff(h2.reshape(rows, d), p["norm_ff2"], p["ff2_gate"], p["ff2_up"], p["ff2_down"], p["norm_final"],
            final_norm=final_norm, tm=tm)
    return y.reshape(bsz, seq, d), k_new, v_new, x_tail[:, -(CONV_WIDTH - 1):], lru_last[:, 0]


def _sample_keys_all_visible(s, cw):
    q_pos = PAST_LEN + np.arange(s)
    k_pos = np.concatenate([PAST_LEN - cw + np.arange(cw), q_pos])
    qch = (q_pos // CHUNK)[:, None]
    kch = (k_pos // CHUNK)[None, :]
    valid = (kch <= qch) & (qch - kch <= WINDOW // CHUNK) & (k_pos[None, :] >= 0)
    return bool(valid.all())


def kernel(x_prompt, x_sample, cache_k, cache_v, state_conv, state_lru, norm_ff1, ff1_gate, ff1_up, ff1_down, norm_mix, w_in, conv_w, conv_b, w_rg, b_rg, w_ig, b_ig, lru_lambda, attn_sinks, w_branch, w_out, norm_ff2, ff2_gate, ff2_up, ff2_down, norm_final):
    depth = w_in.shape[0]
    bp, sp, d = x_prompt.shape
    bs_, ss, _ = x_sample.shape
    d_rnn = conv_w.shape[2]
    cw_len = cache_k.shape[2]
    kv_w = N_KV_HEADS * HEAD_DIM
    assert WINDOW == 2 * CHUNK and cw_len == WINDOW and sp % CHUNK == 0
    assert _sample_keys_all_visible(ss, cw_len)

    hp, hs = x_prompt, x_sample
    outs = [[] for _ in range(8)]
    for l in range(depth):
        p = {
            "norm_ff1": norm_ff1[l][None], "norm_mix": norm_mix[l][None], "norm_ff2": norm_ff2[l][None],
            "norm_final": norm_final[None],
            "ff1_gate": ff1_gate[l].astype(BF16), "ff1_up": ff1_up[l].astype(BF16),
            "ff1_down": ff1_down[l].astype(BF16),
            "ff2_gate": ff2_gate[l].astype(BF16), "ff2_up": ff2_up[l].astype(BF16),
            "ff2_down": ff2_down[l].astype(BF16),
            "w_in": w_in[l].astype(BF16),
            "conv_w": conv_w[l], "conv_b": conv_b[l][None],
            "w_gate": jnp.concatenate([w_rg[l], w_ig[l]], axis=-1).astype(BF16),
            "b_gate": jnp.concatenate([b_rg[l], b_ig[l]], axis=-1)[:, None, :],
            "lru_lambda": lru_lambda[l][None],
            "attn_sinks": attn_sinks[l],
            "w_branch_r": w_branch[l][:d_rnn].astype(BF16), "w_branch_a": w_branch[l][d_rnn:].astype(BF16),
            "w_out": w_out[l].astype(BF16),
        }
        last = l == depth - 1
        zeros = functools.partial(jnp.zeros, dtype=F32)
        hp, k_p, v_p, conv_p, lru_p = _layer(
            hp, zeros((bp, CONV_WIDTH - 1, d_rnn)), zeros((bp, 1, d_rnn)),
            zeros((bp, WINDOW, kv_w)), zeros((bp, WINDOW, kv_w)), p, mask_hist=True, final_norm=last)
        hs, k_s, v_s, conv_s, lru_s = _layer(
            hs, state_conv[l], state_lru[l][:, None, :],
            cache_k[l].reshape(bs_, cw_len, kv_w), cache_v[l].reshape(bs_, cw_len, kv_w), p,
            mask_hist=False, final_norm=last)
        per_layer = (
            k_p.reshape(bp, WINDOW, N_KV_HEADS, HEAD_DIM), v_p.reshape(bp, WINDOW, N_KV_HEADS, HEAD_DIM),
            conv_p, lru_p,
            k_s.reshape(bs_, ss, N_KV_HEADS, HEAD_DIM), v_s.reshape(bs_, ss, N_KV_HEADS, HEAD_DIM),
            conv_s, lru_s,
        )
        for dst, val in zip(outs, per_layer):
            dst.append(val)
    return (hp, hs) + tuple(jnp.stack(o) for o in outs)
```

```python
import functools

import numpy as np
import jax
import jax.numpy as jnp
from jax import lax
from jax.experimental import pallas as pl
from jax.experimental.pallas import tpu as pltpu

CHUNK = 64
N_HEADS = 16
N_KV_HEADS = 4
HEAD_DIM = 64
Q_GROUP = N_HEADS // N_KV_HEADS
WINDOW = 128
N_LRU_BLOCKS = 8
CONV_WIDTH = 4
LRU_C = 8.0
EPS = 1e-6
NEG_INF = -1e30
PAST_LEN = 1024
ATT_SCALE = HEAD_DIM ** -0.5

V7X_LANES = 128
V7X_SUBLANES = 8
V7X_MXU_DIM = 256
V7X_VMEM_BYTES = 64 * 1024 * 1024

F32 = jnp.float32
BF16 = jnp.bfloat16


def _dot(a, b):
    return jnp.dot(a, b, preferred_element_type=F32)


def _rms(x, w):
    return x * lax.rsqrt(jnp.mean(x * x, axis=-1, keepdims=True) + EPS) * w


def _sigmoid(x):
    return 0.5 * jnp.tanh(0.5 * x) + 0.5


def _sqrt_nonneg(z):
    return jnp.exp(0.5 * jnp.log(z))


def _const_spec(shape):
    zeros = (0,) * len(shape)
    return pl.BlockSpec(shape, lambda *_: zeros, pipeline_mode=pl.Buffered(1))


def _vmem_limit(block_bytes, resident_bytes, temp_bytes):
    need = 2 * block_bytes + resident_bytes + temp_bytes
    return int(min(V7X_VMEM_BYTES - (4 << 20), need))


def _ff_chunks(d_ff):
    step = 2 * V7X_MXU_DIM
    edges = list(range(0, d_ff, step)) + [d_ff]
    return tuple((a, b - a) for a, b in zip(edges[:-1], edges[1:]))


def _ff_body(x_ref, nw_ref, wg_ref, wu_ref, wd_ref, nf_ref, o_ref, *, chunks, final_norm):
    x = x_ref[...]
    xn = _rms(x, nw_ref[...]).astype(BF16)
    acc = None
    for c0, cs in chunks:
        gate = _dot(xn, wg_ref[:, c0:c0 + cs])
        up = _dot(xn, wu_ref[:, c0:c0 + cs])
        hid = (gate * jax.nn.sigmoid(gate) * up).astype(BF16)
        part = _dot(hid, wd_ref[c0:c0 + cs, :])
        acc = part if acc is None else acc + part
    y = x + 0.5 * acc
    if final_norm:
        y = _rms(y, nf_ref[...])
    o_ref[...] = y


def _ff(x, nw, wg, wu, wd, nf, *, final_norm, tm):
    rows, d = x.shape
    d_ff = wg.shape[1]
    assert rows % tm == 0
    chunks = _ff_chunks(d_ff)
    row_spec = pl.BlockSpec((tm, d), lambda i: (i, 0))
    widest = max(cs for _, cs in chunks)
    limit = _vmem_limit(
        block_bytes=2 * tm * d * 4,
        resident_bytes=3 * d * d_ff * 2 + 2 * 8 * d * 4,
        temp_bytes=tm * d * (2 + 4 + 4 + 4) + tm * widest * (4 + 4 + 4 + 2),
    )
    return pl.pallas_call(
        functools.partial(_ff_body, chunks=chunks, final_norm=final_norm),
        grid=(rows // tm,),
        in_specs=[row_spec, _const_spec((1, d)), _const_spec((d, d_ff)), _const_spec((d, d_ff)),
                  _const_spec((d_ff, d)), _const_spec((1, d))],
        out_specs=row_spec,
        out_shape=jax.ShapeDtypeStruct((rows, d), F32),
        compiler_params=pltpu.CompilerParams(dimension_semantics=("parallel",), vmem_limit_bytes=limit),
        name="ff",
    )(x, nw, wg, wu, wd, nf)


def _store_head_pairs(buf, row0, x):
    n = x.shape[0]
    lane = lax.broadcasted_iota(jnp.int32, (n, V7X_LANES), 1)
    for kv in range(N_KV_HEADS):
        blk = x[:, V7X_LANES * (kv // 2):V7X_LANES * (kv // 2 + 1)]
        swapped = pltpu.roll(blk, HEAD_DIM, 1)
        keep = (lane < HEAD_DIM) if kv % 2 == 0 else (lane >= HEAD_DIM)
        buf[kv, row0:row0 + n, :] = jnp.where(keep, blk, swapped).astype(BF16)


def _causal_conv(x, tail8, cw, cb):
    sub8 = lax.broadcasted_iota(jnp.int32, tail8.shape, 0)

    def delayed(k):
        full = pltpu.roll(x, k, 0)
        head = jnp.where(sub8 < k, pltpu.roll(tail8, k, 0), full[0:V7X_SUBLANES])
        return jnp.concatenate([head, full[V7X_SUBLANES:]], axis=0)

    out = cb + cw[0:1] * delayed(3)
    out = out + cw[1:2] * delayed(2)
    out = out + cw[2:3] * delayed(1)
    return out + cw[3:4] * x


def _lru_scan(a, b, h0):
    rows = a.shape[0]
    sub = lax.broadcasted_iota(jnp.int32, (V7X_SUBLANES, a.shape[1]), 0)
    out = []
    carry = h0
    for r0 in range(0, rows, V7X_SUBLANES):
        ag = a[r0:r0 + V7X_SUBLANES]
        bg = b[r0:r0 + V7X_SUBLANES]
        shift = 1
        while shift < V7X_SUBLANES:
            take = sub >= shift
            a_prev = pltpu.roll(ag, shift, 0)
            b_prev = pltpu.roll(bg, shift, 0)
            bg = jnp.where(take, ag * b_prev + bg, bg)
            ag = jnp.where(take, ag * a_prev, ag)
            shift *= 2
        hg = ag * carry + bg
        out.append(hg)
        carry = hg[V7X_SUBLANES - 1:V7X_SUBLANES]
    return jnp.concatenate(out, axis=0)


def _attend_group(lhs, keys, vals, sinks, first_valid, filler):
    nt = (((1,), (1,)), ((), ()))
    tn = (((0,), (0,)), ((), ()))
    s = [lax.dot_general(k, q, nt, preferred_element_type=F32) for k, q in zip(keys, lhs)]
    if filler is not None:
        filler()
    if first_valid is not None:
        row = lax.broadcasted_iota(jnp.int32, s[0].shape, 0)
        s = [jnp.where(row >= first_valid, x, NEG_INF) for x in s]
    m = [jnp.maximum(jnp.max(x, axis=0, keepdims=True), sk) for x, sk in zip(s, sinks)]
    p = [jnp.exp(x - mx) for x, mx in zip(s, m)]
    denom = [jnp.sum(x, axis=0, keepdims=True) + jnp.exp(sk - mx) for x, sk, mx in zip(p, sinks, m)]
    pn = [(x * (1.0 / dn)).astype(BF16) for x, dn in zip(p, denom)]
    out = [lax.dot_general(x, v, tn, preferred_element_type=F32) for x, v in zip(pn, vals)]
    return out


def _conv_and_gate_logits(xr, cw_ref, cb_ref, wgate_ref, bgate_ref, tailbuf, xtail_ref, *, nb, tt, d_rnn):
    cw = cw_ref[...]
    cb = cb_ref[...]
    xc_rows = []
    for b in range(nb):
        xrb = xr[b * tt:(b + 1) * tt]
        xc_rows.append(_causal_conv(xrb, tailbuf[b], cw, cb))
        tailbuf[b] = xrb[tt - V7X_SUBLANES:tt]
        xtail_ref[b] = xrb[tt - V7X_SUBLANES:tt]
    xc = jnp.concatenate(xc_rows, axis=0)
    xcb = xc.astype(BF16)
    blk = d_rnn // N_LRU_BLOCKS
    r_parts, i_parts = [], []
    for n in range(N_LRU_BLOCKS):
        z = _dot(xcb[:, n * blk:(n + 1) * blk], wgate_ref[n]) + bgate_ref[n]
        r_parts.append(z[:, :blk])
        i_parts.append(z[:, blk:])
    return xc, jnp.concatenate(r_parts, axis=1), jnp.concatenate(i_parts, axis=1)


def _lru_coefficients(xc, r_pre, i_pre, lam_ref):
    neg_lam = -lam_ref[...]
    softplus = jnp.maximum(neg_lam, 0.0) + jnp.log1p(jnp.exp(-jnp.abs(neg_lam)))
    half_c = (0.5 * LRU_C) * softplus
    neg_log_a = half_c * jnp.tanh(0.5 * r_pre) + half_c
    a = jnp.exp(-neg_log_a)
    return a, _sqrt_nonneg(jnp.tanh(neg_log_a) * (a * a + 1.0)) * (_sigmoid(i_pre) * xc)


def _lru_states(a, bterm, hstate, lru_ref, *, nb, tt):
    h_rows = []
    for b in range(nb):
        hb = _lru_scan(a[b * tt:(b + 1) * tt], bterm[b * tt:(b + 1) * tt], hstate[b])
        hstate[b] = hb[tt - 1:tt]
        lru_ref[b] = hb[tt - 1:tt]
        h_rows.append(hb)
    return jnp.concatenate(h_rows, axis=0)


def _attention(u, win_ref, sinks_ref, kbuf, vbuf, knew_ref, vnew_ref, t, fillers, *, nb, tt, qc, hist, keep,
               mask_hist, c_q, c_kv, c_gr, kv_w):
    q = (_dot(u, win_ref[:, c_q:c_kv]) * ATT_SCALE).astype(BF16)
    kv_new = _dot(u, win_ref[:, c_kv:c_gr])
    k_new = kv_new[:, :kv_w]
    v_new = kv_new[:, kv_w:]
    lane_row = lax.broadcasted_iota(jnp.int32, (1, V7X_LANES), 1)
    sel_lo = jnp.where(lane_row < HEAD_DIM, 1.0, 0.0).astype(BF16)
    sel_hi = jnp.where(lane_row >= HEAD_DIM, 1.0, 0.0).astype(BF16)
    lo = lax.broadcasted_iota(jnp.int32, (qc, V7X_LANES), 1) < HEAD_DIM
    group_of_lane = lax.broadcasted_iota(jnp.int32, (1, Q_GROUP * qc), 1) // qc
    sinks = []
    for kv in range(N_KV_HEADS):
        row = jnp.full((1, Q_GROUP * qc), sinks_ref[kv * Q_GROUP], F32)
        for g in range(1, Q_GROUP):
            row = jnp.where(group_of_lane == g, sinks_ref[kv * Q_GROUP + g], row)
        sinks.append(row)
    nkeys = hist + qc
    fillers = list(fillers)
    att_rows = []
    for b in range(nb):
        kb = kbuf.at[b]
        vb = vbuf.at[b]
        _store_head_pairs(kb, hist, k_new[b * tt:(b + 1) * tt])
        _store_head_pairs(vb, hist, v_new[b * tt:(b + 1) * tt])
        knew_ref[b] = k_new[(b + 1) * tt - keep:(b + 1) * tt]
        vnew_ref[b] = v_new[(b + 1) * tt - keep:(b + 1) * tt]
        for j in range(tt // qc):
            q0 = j * qc
            first_valid = jnp.where(t == 0, hist - q0, 0) if (mask_hist and q0 < hist) else None
            lhs = []
            for kv in range(N_KV_HEADS):
                c0 = kv * Q_GROUP * HEAD_DIM
                qa = q[b * tt + q0:b * tt + q0 + qc, c0:c0 + V7X_LANES]
                qb = q[b * tt + q0:b * tt + q0 + qc, c0 + V7X_LANES:c0 + 2 * V7X_LANES]
                lhs.append(jnp.concatenate([qa * sel_lo, qa * sel_hi, qb * sel_lo, qb * sel_hi], axis=0))
            outs = _attend_group(
                lhs, [kb[kv, q0:q0 + nkeys, :] for kv in range(N_KV_HEADS)],
                [vb[kv, q0:q0 + nkeys, :] for kv in range(N_KV_HEADS)], sinks, first_valid,
                fillers.pop(0) if fillers else None)
            heads = []
            for o in outs:
                heads.append(jnp.where(lo, o[0:qc], o[qc:2 * qc]))
                heads.append(jnp.where(lo, o[2 * qc:3 * qc], o[3 * qc:4 * qc]))
            att_rows.append(jnp.concatenate(heads, axis=1))
        if tt >= hist:
            for kv in range(N_KV_HEADS):
                kb[kv, 0:hist, :] = kb[kv, tt:tt + hist, :]
                vb[kv, 0:hist, :] = vb[kv, tt:tt + hist, :]
    for f in fillers:
        f()
    return jnp.concatenate(att_rows, axis=0).astype(BF16)


def _load_mix_state(conv0_ref, lru0_ref, k0_ref, v0_ref, tailbuf, hstate, kbuf, vbuf, nb):
    tailbuf[...] = jnp.zeros_like(tailbuf)
    tailbuf[:, V7X_SUBLANES - (CONV_WIDTH - 1):V7X_SUBLANES, :] = conv0_ref[...]
    hstate[...] = lru0_ref[...]
    for b in range(nb):
        _store_head_pairs(kbuf.at[b], 0, k0_ref[b])
        _store_head_pairs(vbuf.at[b], 0, v0_ref[b])


def _mix_core(hin, t, side_work, sinks_ref, nw_ref, win_ref, cw_ref, cb_ref, wgate_ref, bgate_ref, lam_ref, wbr_ref,
              wba_ref, wout_ref, lru_ref, xtail_ref, knew_ref, vnew_ref, tailbuf, hstate, kbuf, vbuf,
              *, nb, tt, qc, hist, keep, mask_hist, d_rnn, att_w, kv_w):
    d = hin.shape[-1]
    side = list(side_work)

    def run_side():
        if side:
            side.pop(0)()

    u = _rms(hin, nw_ref[...]).astype(BF16)
    c_g = d_rnn
    c_q = c_g + d_rnn
    c_kv = c_q + att_w
    c_gr = c_kv + 2 * kv_w
    c_ga = c_gr + d
    xr = _dot(u, win_ref[:, 0:c_g])
    st = {}

    def stage_conv():
        run_side()
        st["xc"], st["r_pre"], st["i_pre"] = _conv_and_gate_logits(
            xr, cw_ref, cb_ref, wgate_ref, bgate_ref, tailbuf, xtail_ref, nb=nb, tt=tt, d_rnn=d_rnn)
        st["g_rnn"] = _dot(u, win_ref[:, c_g:c_q])

    def stage_coefficients():
        run_side()
        st["a"], st["b"] = _lru_coefficients(st["xc"], st["r_pre"], st["i_pre"], lam_ref)
        st["gate_r"] = _sigmoid(_dot(u, win_ref[:, c_gr:c_ga]))

    def stage_scan():
        run_side()
        st["h"] = _lru_states(st["a"], st["b"], hstate, lru_ref, nb=nb, tt=tt)
        st["gate_a"] = _sigmoid(_dot(u, win_ref[:, c_ga:c_ga + d]))

    def stage_branch():
        run_side()
        rec = (jax.nn.gelu(st["g_rnn"]) * st["h"]).astype(BF16)
        st["gated_r"] = st["gate_r"] * _dot(rec, wbr_ref[...])

    att = _attention(u, win_ref, sinks_ref, kbuf, vbuf, knew_ref, vnew_ref, t,
                     [stage_conv, stage_coefficients, stage_scan, stage_branch], nb=nb, tt=tt, qc=qc,
                     hist=hist, keep=keep, mask_hist=mask_hist, c_q=c_q, c_kv=c_kv, c_gr=c_gr, kv_w=kv_w)
    run_side()
    merged = st["gated_r"] + st["gate_a"] * _dot(att, wba_ref[...])
    while side:
        run_side()
    return hin + _dot(merged.astype(BF16), wout_ref[...])


def _mix_body(sinks_ref, h_ref, conv0_ref, lru0_ref, k0_ref, v0_ref,
              nw_ref, win_ref, cw_ref, cb_ref, wgate_ref, bgate_ref, lam_ref, wbr_ref, wba_ref, wout_ref,
              h2_ref, lru_ref, xtail_ref, knew_ref, vnew_ref,
              tailbuf, hstate, kbuf, vbuf, *, nb, tt, **dims):
    t = pl.program_id(1)
    d = h_ref.shape[-1]

    @pl.when(t == 0)
    def _load_state():
        _load_mix_state(conv0_ref, lru0_ref, k0_ref, v0_ref, tailbuf, hstate, kbuf, vbuf, nb)

    h2 = _mix_core(h_ref[...].reshape(nb * tt, d), t, [], sinks_ref, nw_ref, win_ref, cw_ref, cb_ref, wgate_ref,
                   bgate_ref, lam_ref, wbr_ref, wba_ref, wout_ref, lru_ref, xtail_ref, knew_ref, vnew_ref,
                   tailbuf, hstate, kbuf, vbuf, nb=nb, tt=tt, **dims)
    h2_ref[...] = h2.reshape(nb, tt, d)


def _ffmix_body(sinks_ref, x_ref, conv0_ref, lru0_ref, k0_ref, v0_ref, nff_ref, wg_ref, wu_ref, wd_ref,
                nw_ref, win_ref, cw_ref, cb_ref, wgate_ref, bgate_ref, lam_ref, wbr_ref, wba_ref, wout_ref,
                h2_ref, lru_ref, xtail_ref, knew_ref, vnew_ref,
                hbuf, tailbuf, hstate, kbuf, vbuf, *, n_t, chunks, tt, **dims):
    s = pl.program_id(0)
    t = lax.rem(s + (n_t - 1), n_t)

    @pl.when(s == 0)
    def _zero_scratch():
        hbuf[...] = jnp.zeros_like(hbuf)
        tailbuf[...] = jnp.zeros_like(tailbuf)
        hstate[...] = jnp.zeros_like(hstate)
        kbuf[...] = jnp.zeros_like(kbuf)
        vbuf[...] = jnp.zeros_like(vbuf)

    @pl.when(jnp.logical_and(t == 0, s > 0))
    def _load_state():
        _load_mix_state(conv0_ref, lru0_ref, k0_ref, v0_ref, tailbuf, hstate, kbuf, vbuf, 1)

    x = x_ref[0]
    xn = _rms(x, nff_ref[...]).astype(BF16)
    acc = []

    def ff_chunk(c0, cs):
        def run():
            gate = _dot(xn, wg_ref[:, c0:c0 + cs])
            up = _dot(xn, wu_ref[:, c0:c0 + cs])
            hid = (gate * jax.nn.sigmoid(gate) * up).astype(BF16)
            part = _dot(hid, wd_ref[c0:c0 + cs, :])
            acc[:] = [part if not acc else acc[0] + part]
        return run

    h2 = _mix_core(hbuf[...], t, [ff_chunk(c0, cs) for c0, cs in chunks], sinks_ref, nw_ref, win_ref, cw_ref,
                   cb_ref, wgate_ref, bgate_ref, lam_ref, wbr_ref, wba_ref, wout_ref, lru_ref, xtail_ref,
                   knew_ref, vnew_ref, tailbuf, hstate, kbuf, vbuf, nb=1, tt=tt, **dims)
    h2_ref[0] = h2
    hbuf[...] = x + 0.5 * acc[0]


def _mix(sinks, h, conv0, lru0, k0, v0, nw, win, cw, cb, wgate, bgate, lam, wbr, wba, wout,
         *, nb, tt, qc, mask_hist):
    bsz, seq, d = h.shape
    d_rnn = cw.shape[1]
    att_w = N_HEADS * HEAD_DIM
    kv_w = N_KV_HEADS * HEAD_DIM
    hist = k0.shape[1]
    keep = min(tt, hist)
    n_t = seq // tt
    assert seq % tt == 0 and bsz % nb == 0 and tt % qc == 0 and qc % 16 == 0
    assert n_t == 1 or tt >= hist
    assert 2 * HEAD_DIM == V7X_LANES and win.shape[1] == 2 * d_rnn + att_w + 2 * kv_w + 2 * d

    def bs(r, w, follow_t=False):
        return pl.BlockSpec((nb, r, w), (lambda b, t: (b, t, 0)) if follow_t else (lambda b, t: (b, 0, 0)))

    m = nb * tt
    weight_bytes = (win.size + wgate.size + wbr.size + wba.size + wout.size) * 2
    limit = _vmem_limit(
        block_bytes=2 * m * d * 4 + nb * (8 * d_rnn + 4 * hist * kv_w) * 4,
        resident_bytes=weight_bytes + 2 * nb * N_KV_HEADS * (hist + tt) * V7X_LANES * 2 + nb * 16 * d_rnn * 4,
        temp_bytes=m * win.shape[1] * 4 + 12 * m * d_rnn * 4,
    )
    out_shape = [jax.ShapeDtypeStruct((bsz, seq, d), F32), jax.ShapeDtypeStruct((bsz, 1, d_rnn), F32),
                 jax.ShapeDtypeStruct((bsz, V7X_SUBLANES, d_rnn), F32),
                 jax.ShapeDtypeStruct((bsz, keep, kv_w), F32), jax.ShapeDtypeStruct((bsz, keep, kv_w), F32)]
    return pl.pallas_call(
        functools.partial(_mix_body, nb=nb, tt=tt, qc=qc, hist=hist, keep=keep, mask_hist=mask_hist,
                          d_rnn=d_rnn, att_w=att_w, kv_w=kv_w),
        grid=(bsz // nb, n_t),
        in_specs=[pl.BlockSpec(memory_space=pltpu.SMEM),
                  bs(tt, d, follow_t=True), bs(CONV_WIDTH - 1, d_rnn), bs(1, d_rnn), bs(hist, kv_w),
                  bs(hist, kv_w)]
        + [_const_spec(w.shape) for w in (nw, win, cw, cb, wgate, bgate, lam, wbr, wba, wout)],
        out_specs=[bs(tt, d, follow_t=True), bs(1, d_rnn), bs(V7X_SUBLANES, d_rnn), bs(keep, kv_w),
                   bs(keep, kv_w)],
        out_shape=out_shape,
        scratch_shapes=[pltpu.VMEM((nb, V7X_SUBLANES, d_rnn), F32),
                        pltpu.VMEM((nb, 1, d_rnn), F32),
                        pltpu.VMEM((nb, N_KV_HEADS, hist + tt, V7X_LANES), BF16),
                        pltpu.VMEM((nb, N_KV_HEADS, hist + tt, V7X_LANES), BF16)],
        compiler_params=pltpu.CompilerParams(dimension_semantics=("parallel", "arbitrary"),
                                             vmem_limit_bytes=limit),
        name="mix",
    )(sinks, h, conv0, lru0, k0, v0, nw, win, cw, cb, wgate, bgate, lam, wbr, wba, wout)


def _ffmix(sinks, x, conv0, lru0, k0, v0, nff, wg, wu, wd, nw, win, cw, cb, wgate, bgate, lam, wbr, wba, wout,
           *, tt, qc, mask_hist):
    bsz, seq, d = x.shape
    d_rnn = cw.shape[1]
    d_ff = wg.shape[1]
    att_w = N_HEADS * HEAD_DIM
    kv_w = N_KV_HEADS * HEAD_DIM
    hist = k0.shape[1]
    n_t = seq // tt
    n_tiles = bsz * n_t
    assert seq % tt == 0 and tt % qc == 0 and qc % 16 == 0 and tt >= hist and n_t > 1
    assert 2 * HEAD_DIM == V7X_LANES and win.shape[1] == 2 * d_rnn + att_w + 2 * kv_w + 2 * d

    def ff_tile(s):
        i = jnp.minimum(s, n_tiles - 1)
        return i // n_t, i % n_t, 0

    def mix_tile(s):
        i = jnp.maximum(s - 1, 0)
        return i // n_t, i % n_t, 0

    def mix_row(s):
        return jnp.maximum(s - 1, 0) // n_t, 0, 0

    def row_spec(r, w):
        return pl.BlockSpec((1, r, w), mix_row)

    weights = (nff, wg, wu, wd, nw, win, cw, cb, wgate, bgate, lam, wbr, wba, wout)
    weight_bytes = (3 * d * d_ff + win.size + wgate.size + wbr.size + wba.size + wout.size) * 2
    limit = _vmem_limit(
        block_bytes=2 * tt * d * 4 + (8 * d_rnn + 4 * hist * kv_w) * 4,
        resident_bytes=weight_bytes + tt * d * 4 + 2 * N_KV_HEADS * (hist + tt) * V7X_LANES * 2 + 16 * d_rnn * 4,
        temp_bytes=tt * win.shape[1] * 4 + 16 * tt * d_rnn * 4,
    )
    out_shape = [jax.ShapeDtypeStruct((bsz, seq, d), F32), jax.ShapeDtypeStruct((bsz, 1, d_rnn), F32),
                 jax.ShapeDtypeStruct((bsz, V7X_SUBLANES, d_rnn), F32),
                 jax.ShapeDtypeStruct((bsz, hist, kv_w), F32), jax.ShapeDtypeStruct((bsz, hist, kv_w), F32)]
    return pl.pallas_call(
        functools.partial(_ffmix_body, n_t=n_t, chunks=_ff_chunks(d_ff), tt=tt, qc=qc, hist=hist, keep=hist,
                          mask_hist=mask_hist, d_rnn=d_rnn, att_w=att_w, kv_w=kv_w),
        grid=(n_tiles + 1,),
        in_specs=[pl.BlockSpec(memory_space=pltpu.SMEM), pl.BlockSpec((1, tt, d), ff_tile),
                  row_spec(CONV_WIDTH - 1, d_rnn), row_spec(1, d_rnn), row_spec(hist, kv_w), row_spec(hist, kv_w)]
        + [_const_spec(w.shape) for w in weights],
        out_specs=[pl.BlockSpec((1, tt, d), mix_tile), row_spec(1, d_rnn), row_spec(V7X_SUBLANES, d_rnn),
                   row_spec(hist, kv_w), row_spec(hist, kv_w)],
        out_shape=out_shape,
        scratch_shapes=[pltpu.VMEM((tt, d), F32),
                        pltpu.VMEM((1, V7X_SUBLANES, d_rnn), F32),
                        pltpu.VMEM((1, 1, d_rnn), F32),
                        pltpu.VMEM((1, N_KV_HEADS, hist + tt, V7X_LANES), BF16),
                        pltpu.VMEM((1, N_KV_HEADS, hist + tt, V7X_LANES), BF16)],
        compiler_params=pltpu.CompilerParams(dimension_semantics=("arbitrary",), vmem_limit_bytes=limit),
        name="ffmix",
    )(sinks, x, conv0, lru0, k0, v0, *weights)


def _tiles(bsz, seq):
    if seq % 256 == 0:
        return 512, 1, 256, CHUNK
    assert seq <= CHUNK and (bsz * seq) % V7X_SUBLANES == 0
    return bsz * seq, bsz, seq, seq


def _layer(x, conv0, lru0, k0, v0, p, *, mask_hist, final_norm):
    bsz, seq, d = x.shape
    rows = bsz * seq
    tm, nb, tt, qc = _tiles(bsz, seq)
    mix_weights = (p["norm_mix"], p["w_in"], p["conv_w"], p["conv_b"], p["w_gate"], p["b_gate"], p["lru_lambda"],
                   p["w_branch_r"], p["w_branch_a"], p["w_out"])
    if seq > tt:
        h2, lru_last, x_tail, k_new, v_new = _ffmix(
            p["attn_sinks"], x, conv0, lru0, k0, v0, p["norm_ff1"], p["ff1_gate"], p["ff1_up"], p["ff1_down"],
            *mix_weights, tt=tt, qc=qc, mask_hist=mask_hist)
    else:
        h = _ff(x.reshape(rows, d), p["norm_ff1"], p["ff1_gate"], p["ff1_up"], p["ff1_down"], p["norm_final"],
                final_norm=False, tm=tm)
        h2, lru_last, x_tail, k_new, v_new = _mix(
            p["attn_sinks"], h.reshape(bsz, seq, d), conv0, lru0, k0, v0, *mix_weights,
            nb=nb, tt=tt, qc=qc, mask_hist=mask_hist)
    y = _ff(h2.reshape(rows, d), p["norm_ff2"], p["ff2_gate"], p["ff2_up"], p["ff2_down"], p["norm_final"],
            final_norm=final_norm, tm=tm)
    return y.reshape(bsz, seq, d), k_new, v_new, x_tail[:, -(CONV_WIDTH - 1):], lru_last[:, 0]


def _sample_keys_all_visible(s, cw):
    q_pos = PAST_LEN + np.arange(s)
    k_pos = np.concatenate([PAST_LEN - cw + np.arange(cw), q_pos])
    qch = (q_pos // CHUNK)[:, None]
    kch = (k_pos // CHUNK)[None, :]
    valid = (kch <= qch) & (qch - kch <= WINDOW // CHUNK) & (k_pos[None, :] >= 0)
    return bool(valid.all())


def kernel(x_prompt, x_sample, cache_k, cache_v, state_conv, state_lru, norm_ff1, ff1_gate, ff1_up, ff1_down, norm_mix, w_in, conv_w, conv_b, w_rg, b_rg, w_ig, b_ig, lru_lambda, attn_sinks, w_branch, w_out, norm_ff2, ff2_gate, ff2_up, ff2_down, norm_final):
    depth = w_in.shape[0]
    bp, sp, d = x_prompt.shape
    bs_, ss, _ = x_sample.shape
    d_rnn = conv_w.shape[2]
    cw_len = cache_k.shape[2]
    kv_w = N_KV_HEADS * HEAD_DIM
    assert WINDOW == 2 * CHUNK and cw_len == WINDOW and sp % CHUNK == 0
    assert _sample_keys_all_visible(ss, cw_len)

    hp, hs = x_prompt, x_sample
    outs = [[] for _ in range(8)]
    for l in range(depth):
        p = {
            "norm_ff1": norm_ff1[l][None], "norm_mix": norm_mix[l][None], "norm_ff2": norm_ff2[l][None],
            "norm_final": norm_final[None],
            "ff1_gate": ff1_gate[l].astype(BF16), "ff1_up": ff1_up[l].astype(BF16),
            "ff1_down": ff1_down[l].astype(BF16),
            "ff2_gate": ff2_gate[l].astype(BF16), "ff2_up": ff2_up[l].astype(BF16),
            "ff2_down": ff2_down[l].astype(BF16),
            "w_in": w_in[l].astype(BF16),
            "conv_w": conv_w[l], "conv_b": conv_b[l][None],
            "w_gate": jnp.concatenate([w_rg[l], w_ig[l]], axis=-1).astype(BF16),
            "b_gate": jnp.concatenate([b_rg[l], b_ig[l]], axis=-1)[:, None, :],
            "lru_lambda": lru_lambda[l][None],
            "attn_sinks": attn_sinks[l],
            "w_branch_r": w_branch[l][:d_rnn].astype(BF16), "w_branch_a": w_branch[l][d_rnn:].astype(BF16),
            "w_out": w_out[l].astype(BF16),
        }
        last = l == depth - 1
        zeros = functools.partial(jnp.zeros, dtype=F32)
        hp, k_p, v_p, conv_p, lru_p = _layer(
            hp, zeros((bp, CONV_WIDTH - 1, d_rnn)), zeros((bp, 1, d_rnn)),
            zeros((bp, WINDOW, kv_w)), zeros((bp, WINDOW, kv_w)), p, mask_hist=True, final_norm=last)
        hs, k_s, v_s, conv_s, lru_s = _layer(
            hs, state_conv[l], state_lru[l][:, None, :],
            cache_k[l].reshape(bs_, cw_len, kv_w), cache_v[l].reshape(bs_, cw_len, kv_w), p,
            mask_hist=False, final_norm=last)
        per_layer = (
            k_p.reshape(bp, WINDOW, N_KV_HEADS, HEAD_DIM), v_p.reshape(bp, WINDOW, N_KV_HEADS, HEAD_DIM),
            conv_p, lru_p,
            k_s.reshape(bs_, ss, N_KV_HEADS, HEAD_DIM), v_s.reshape(bs_, ss, N_KV_HEADS, HEAD_DIM),
            conv_s, lru_s,
        )
        for dst, val in zip(outs, per_layer):
            dst.append(val)
    return (hp, hs) + tuple(jnp.stack(o) for o in outs)
```

```python
import functools

import numpy as np
import jax
import jax.numpy as jnp
from jax import lax
from jax.experimental import pallas as pl
from jax.experimental.pallas import tpu as pltpu

CHUNK = 64
N_HEADS = 16
N_KV_HEADS = 4
HEAD_DIM = 64
Q_GROUP = N_HEADS // N_KV_HEADS
WINDOW = 128
N_LRU_BLOCKS = 8
CONV_WIDTH = 4
LRU_C = 8.0
EPS = 1e-6
NEG_INF = -1e30
PAST_LEN = 1024
ATT_SCALE = HEAD_DIM ** -0.5

V7X_LANES = 128
V7X_SUBLANES = 8
V7X_MXU_DIM = 256
V7X_VMEM_BYTES = 64 * 1024 * 1024

F32 = jnp.float32
BF16 = jnp.bfloat16


def _dot(a, b):
    return jnp.dot(a, b, preferred_element_type=F32)


def _rms(x, w):
    return x * lax.rsqrt(jnp.mean(x * x, axis=-1, keepdims=True) + EPS) * w


def _sigmoid(x):
    return 0.5 * jnp.tanh(0.5 * x) + 0.5


def _sqrt_nonneg(z):
    return jnp.exp(0.5 * jnp.log(z))


def _const_spec(shape):
    zeros = (0,) * len(shape)
    return pl.BlockSpec(shape, lambda *_: zeros, pipeline_mode=pl.Buffered(1))


def _vmem_limit(block_bytes, resident_bytes, temp_bytes):
    need = 2 * block_bytes + resident_bytes + temp_bytes
    return int(min(V7X_VMEM_BYTES - (4 << 20), need))


def _ff_chunks(d_ff):
    step = 2 * V7X_MXU_DIM
    edges = list(range(0, d_ff, step)) + [d_ff]
    return tuple((a, b - a) for a, b in zip(edges[:-1], edges[1:]))


def _ff_body(x_ref, nw_ref, wg_ref, wu_ref, wd_ref, nf_ref, o_ref, *, chunks, final_norm, sub_rows):
    tm = x_ref.shape[0]

    def normed(r0):
        x = x_ref[r0:r0 + sub_rows, :]
        return x, _rms(x, nw_ref[...]).astype(BF16)

    pending = normed(0)
    for r0 in range(0, tm, sub_rows):
        x, xn = pending
        acc = None
        for ci, (c0, cs) in enumerate(chunks):
            gate = _dot(xn, wg_ref[:, c0:c0 + cs])
            up = _dot(xn, wu_ref[:, c0:c0 + cs])
            hid = (gate * jax.nn.sigmoid(gate) * up).astype(BF16)
            part = _dot(hid, wd_ref[c0:c0 + cs, :])
            acc = part if acc is None else acc + part
            if ci == 0 and r0 + sub_rows < tm:
                pending = normed(r0 + sub_rows)
        y = x + 0.5 * acc
        if final_norm:
            y = _rms(y, nf_ref[...])
        o_ref[r0:r0 + sub_rows, :] = y


def _ff(x, nw, wg, wu, wd, nf, *, final_norm, tm):
    rows, d = x.shape
    d_ff = wg.shape[1]
    assert rows % tm == 0
    sub_rows = min(tm, 2 * V7X_MXU_DIM)
    assert tm % sub_rows == 0
    chunks = _ff_chunks(d_ff)
    row_spec = pl.BlockSpec((tm, d), lambda i: (i, 0))
    widest = max(cs for _, cs in chunks)
    limit = _vmem_limit(
        block_bytes=2 * tm * d * 4,
        resident_bytes=3 * d * d_ff * 2 + 2 * 8 * d * 4,
        temp_bytes=2 * sub_rows * (d * (2 + 4 + 4 + 4) + widest * (4 + 4 + 4 + 2)),
    )
    return pl.pallas_call(
        functools.partial(_ff_body, chunks=chunks, final_norm=final_norm, sub_rows=sub_rows),
        grid=(rows // tm,),
        in_specs=[row_spec, _const_spec((1, d)), _const_spec((d, d_ff)), _const_spec((d, d_ff)),
                  _const_spec((d_ff, d)), _const_spec((1, d))],
        out_specs=row_spec,
        out_shape=jax.ShapeDtypeStruct((rows, d), F32),
        compiler_params=pltpu.CompilerParams(dimension_semantics=("parallel",), vmem_limit_bytes=limit),
        name="ff",
    )(x, nw, wg, wu, wd, nf)


def _store_head_pairs(buf, row0, x):
    n = x.shape[0]
    lane = lax.broadcasted_iota(jnp.int32, (n, V7X_LANES), 1)
    for kv in range(N_KV_HEADS):
        blk = x[:, V7X_LANES * (kv // 2):V7X_LANES * (kv // 2 + 1)]
        swapped = pltpu.roll(blk, HEAD_DIM, 1)
        keep = (lane < HEAD_DIM) if kv % 2 == 0 else (lane >= HEAD_DIM)
        buf[kv, row0:row0 + n, :] = jnp.where(keep, blk, swapped).astype(BF16)


def _causal_conv(x, tail8, cw, cb):
    sub8 = lax.broadcasted_iota(jnp.int32, tail8.shape, 0)

    def delayed(k):
        full = pltpu.roll(x, k, 0)
        head = jnp.where(sub8 < k, pltpu.roll(tail8, k, 0), full[0:V7X_SUBLANES])
        return jnp.concatenate([head, full[V7X_SUBLANES:]], axis=0)

    out = cb + cw[0:1] * delayed(3)
    out = out + cw[1:2] * delayed(2)
    out = out + cw[2:3] * delayed(1)
    return out + cw[3:4] * x


def _lru_scan(a, b, h0):
    rows = a.shape[0]
    sub = lax.broadcasted_iota(jnp.int32, (V7X_SUBLANES, a.shape[1]), 0)
    out = []
    carry = h0
    for r0 in range(0, rows, V7X_SUBLANES):
        ag = a[r0:r0 + V7X_SUBLANES]
        bg = b[r0:r0 + V7X_SUBLANES]
        shift = 1
        while shift < V7X_SUBLANES:
            take = sub >= shift
            a_prev = pltpu.roll(ag, shift, 0)
            b_prev = pltpu.roll(bg, shift, 0)
            bg = jnp.where(take, ag * b_prev + bg, bg)
            ag = jnp.where(take, ag * a_prev, ag)
            shift *= 2
        hg = ag * carry + bg
        out.append(hg)
        carry = hg[V7X_SUBLANES - 1:V7X_SUBLANES]
    return jnp.concatenate(out, axis=0)


def _attend_group(lhs, keys, vals, sinks, first_valid, filler):
    nt = (((1,), (1,)), ((), ()))
    tn = (((0,), (0,)), ((), ()))
    s = [lax.dot_general(k, q, nt, preferred_element_type=F32) for k, q in zip(keys, lhs)]
    if filler is not None:
        filler()
    if first_valid is not None:
        row = lax.broadcasted_iota(jnp.int32, s[0].shape, 0)
        s = [jnp.where(row >= first_valid, x, NEG_INF) for x in s]
    m = [jnp.maximum(jnp.max(x, axis=0, keepdims=True), sk) for x, sk in zip(s, sinks)]
    p = [jnp.exp(x - mx) for x, mx in zip(s, m)]
    denom = [jnp.sum(x, axis=0, keepdims=True) + jnp.exp(sk - mx) for x, sk, mx in zip(p, sinks, m)]
    pn = [(x * (1.0 / dn)).astype(BF16) for x, dn in zip(p, denom)]
    out = [lax.dot_general(x, v, tn, preferred_element_type=F32) for x, v in zip(pn, vals)]
    return out


def _conv_and_gate_logits(xr, cw_ref, cb_ref, wgate_ref, bgate_ref, tailbuf, xtail_ref, *, nb, tt, d_rnn):
    cw = cw_ref[...]
    cb = cb_ref[...]
    xc_rows = []
    for b in range(nb):
        xrb = xr[b * tt:(b + 1) * tt]
        xc_rows.append(_causal_conv(xrb, tailbuf[b], cw, cb))
        tailbuf[b] = xrb[tt - V7X_SUBLANES:tt]
        xtail_ref[b] = xrb[tt - V7X_SUBLANES:tt]
    xc = jnp.concatenate(xc_rows, axis=0)
    xcb = xc.astype(BF16)
    blk = d_rnn // N_LRU_BLOCKS
    r_parts, i_parts = [], []
    for n in range(N_LRU_BLOCKS):
        z = _dot(xcb[:, n * blk:(n + 1) * blk], wgate_ref[n]) + bgate_ref[n]
        r_parts.append(z[:, :blk])
        i_parts.append(z[:, blk:])
    return xc, jnp.concatenate(r_parts, axis=1), jnp.concatenate(i_parts, axis=1)


def _lru_coefficients(xc, r_pre, i_pre, lam_ref):
    neg_lam = -lam_ref[...]
    softplus = jnp.maximum(neg_lam, 0.0) + jnp.log1p(jnp.exp(-jnp.abs(neg_lam)))
    half_c = (0.5 * LRU_C) * softplus
    neg_log_a = half_c * jnp.tanh(0.5 * r_pre) + half_c
    a = jnp.exp(-neg_log_a)
    return a, _sqrt_nonneg(jnp.tanh(neg_log_a) * (a * a + 1.0)) * (_sigmoid(i_pre) * xc)


def _lru_states(a, bterm, hstate, lru_ref, *, nb, tt):
    h_rows = []
    for b in range(nb):
        hb = _lru_scan(a[b * tt:(b + 1) * tt], bterm[b * tt:(b + 1) * tt], hstate[b])
        hstate[b] = hb[tt - 1:tt]
        lru_ref[b] = hb[tt - 1:tt]
        h_rows.append(hb)
    return jnp.concatenate(h_rows, axis=0)


def _attention(u, win_ref, sinks_ref, kbuf, vbuf, knew_ref, vnew_ref, t, fillers, *, nb, tt, qc, hist, keep,
               mask_hist, c_q, c_kv, c_gr, kv_w):
    q = (_dot(u, win_ref[:, c_q:c_kv]) * ATT_SCALE).astype(BF16)
    kv_new = _dot(u, win_ref[:, c_kv:c_gr])
    k_new = kv_new[:, :kv_w]
    v_new = kv_new[:, kv_w:]
    lane_row = lax.broadcasted_iota(jnp.int32, (1, V7X_LANES), 1)
    sel_lo = jnp.where(lane_row < HEAD_DIM, 1.0, 0.0).astype(BF16)
    sel_hi = jnp.where(lane_row >= HEAD_DIM, 1.0, 0.0).astype(BF16)
    lo = lax.broadcasted_iota(jnp.int32, (qc, V7X_LANES), 1) < HEAD_DIM
    group_of_lane = lax.broadcasted_iota(jnp.int32, (1, Q_GROUP * qc), 1) // qc
    sinks = []
    for kv in range(N_KV_HEADS):
        row = jnp.full((1, Q_GROUP * qc), sinks_ref[kv * Q_GROUP], F32)
        for g in range(1, Q_GROUP):
            row = jnp.where(group_of_lane == g, sinks_ref[kv * Q_GROUP + g], row)
        sinks.append(row)
    nkeys = hist + qc
    fillers = list(fillers)
    att_rows = []
    for b in range(nb):
        kb = kbuf.at[b]
        vb = vbuf.at[b]
        _store_head_pairs(kb, hist, k_new[b * tt:(b + 1) * tt])
        _store_head_pairs(vb, hist, v_new[b * tt:(b + 1) * tt])
        knew_ref[b] = k_new[(b + 1) * tt - keep:(b + 1) * tt]
        vnew_ref[b] = v_new[(b + 1) * tt - keep:(b + 1) * tt]
        for j in range(tt // qc):
            q0 = j * qc
            first_valid = jnp.where(t == 0, hist - q0, 0) if (mask_hist and q0 < hist) else None
            lhs = []
            for kv in range(N_KV_HEADS):
                c0 = kv * Q_GROUP * HEAD_DIM
                qa = q[b * tt + q0:b * tt + q0 + qc, c0:c0 + V7X_LANES]
                qb = q[b * tt + q0:b * tt + q0 + qc, c0 + V7X_LANES:c0 + 2 * V7X_LANES]
                lhs.append(jnp.concatenate([qa * sel_lo, qa * sel_hi, qb * sel_lo, qb * sel_hi], axis=0))
            outs = _attend_group(
                lhs, [kb[kv, q0:q0 + nkeys, :] for kv in range(N_KV_HEADS)],
                [vb[kv, q0:q0 + nkeys, :] for kv in range(N_KV_HEADS)], sinks, first_valid,
                fillers.pop(0) if fillers else None)
            heads = []
            for o in outs:
                heads.append(jnp.where(lo, o[0:qc], o[qc:2 * qc]))
                heads.append(jnp.where(lo, o[2 * qc:3 * qc], o[3 * qc:4 * qc]))
            att_rows.append(jnp.concatenate(heads, axis=1))
        if tt >= hist:
            for kv in range(N_KV_HEADS):
                kb[kv, 0:hist, :] = kb[kv, tt:tt + hist, :]
                vb[kv, 0:hist, :] = vb[kv, tt:tt + hist, :]
    for f in fillers:
        f()
    return jnp.concatenate(att_rows, axis=0).astype(BF16)


def _mix_body(sinks_ref, h_ref, conv0_ref, lru0_ref, k0_ref, v0_ref,
              nw_ref, win_ref, cw_ref, cb_ref, wgate_ref, bgate_ref, lam_ref, wbr_ref, wba_ref, wout_ref,
              h2_ref, lru_ref, xtail_ref, knew_ref, vnew_ref,
              tailbuf, hstate, kbuf, vbuf, *, nb, tt, qc, hist, keep, mask_hist, d_rnn, att_w, kv_w):
    t = pl.program_id(1)
    d = h_ref.shape[-1]
    m = nb * tt

    @pl.when(t == 0)
    def _load_state():
        tailbuf[...] = jnp.zeros_like(tailbuf)
        tailbuf[:, V7X_SUBLANES - (CONV_WIDTH - 1):V7X_SUBLANES, :] = conv0_ref[...]
        hstate[...] = lru0_ref[...]
        for b in range(nb):
            _store_head_pairs(kbuf.at[b], 0, k0_ref[b])
            _store_head_pairs(vbuf.at[b], 0, v0_ref[b])

    hin = h_ref[...].reshape(m, d)
    u = _rms(hin, nw_ref[...]).astype(BF16)
    c_g = d_rnn
    c_q = c_g + d_rnn
    c_kv = c_q + att_w
    c_gr = c_kv + 2 * kv_w
    c_ga = c_gr + d
    xr = _dot(u, win_ref[:, 0:c_g])
    st = {}

    def stage_conv():
        st["xc"], st["r_pre"], st["i_pre"] = _conv_and_gate_logits(
            xr, cw_ref, cb_ref, wgate_ref, bgate_ref, tailbuf, xtail_ref, nb=nb, tt=tt, d_rnn=d_rnn)
        st["g_rnn"] = _dot(u, win_ref[:, c_g:c_q])

    def stage_coefficients():
        st["a"], st["b"] = _lru_coefficients(st["xc"], st["r_pre"], st["i_pre"], lam_ref)
        st["gate_r"] = _sigmoid(_dot(u, win_ref[:, c_gr:c_ga]))

    def stage_scan():
        st["h"] = _lru_states(st["a"], st["b"], hstate, lru_ref, nb=nb, tt=tt)
        st["gate_a"] = _sigmoid(_dot(u, win_ref[:, c_ga:c_ga + d]))

    def stage_branch():
        rec = (jax.nn.gelu(st["g_rnn"]) * st["h"]).astype(BF16)
        st["gated_r"] = st["gate_r"] * _dot(rec, wbr_ref[...])

    att = _attention(u, win_ref, sinks_ref, kbuf, vbuf, knew_ref, vnew_ref, t,
                     [stage_conv, stage_coefficients, stage_scan, stage_branch], nb=nb, tt=tt, qc=qc,
                     hist=hist, keep=keep, mask_hist=mask_hist, c_q=c_q, c_kv=c_kv, c_gr=c_gr, kv_w=kv_w)
    merged = st["gated_r"] + st["gate_a"] * _dot(att, wba_ref[...])
    h2_ref[...] = (hin + _dot(merged.astype(BF16), wout_ref[...])).reshape(nb, tt, d)


def _mix(sinks, h, conv0, lru0, k0, v0, nw, win, cw, cb, wgate, bgate, lam, wbr, wba, wout,
         *, nb, tt, qc, mask_hist):
    bsz, seq, d = h.shape
    d_rnn = cw.shape[1]
    att_w = N_HEADS * HEAD_DIM
    kv_w = N_KV_HEADS * HEAD_DIM
    hist = k0.shape[1]
    keep = min(tt, hist)
    n_t = seq // tt
    assert seq % tt == 0 and bsz % nb == 0 and tt % qc == 0 and qc % 16 == 0
    assert n_t == 1 or tt >= hist
    assert 2 * HEAD_DIM == V7X_LANES and win.shape[1] == 2 * d_rnn + att_w + 2 * kv_w + 2 * d

    def bs(r, w, follow_t=False):
        return pl.BlockSpec((nb, r, w), (lambda b, t: (b, t, 0)) if follow_t else (lambda b, t: (b, 0, 0)))

    m = nb * tt
    weight_bytes = (win.size + wgate.size + wbr.size + wba.size + wout.size) * 2
    limit = _vmem_limit(
        block_bytes=2 * m * d * 4 + nb * (8 * d_rnn + 4 * hist * kv_w) * 4,
        resident_bytes=weight_bytes + 2 * nb * N_KV_HEADS * (hist + tt) * V7X_LANES * 2 + nb * 16 * d_rnn * 4,
        temp_bytes=m * win.shape[1] * 4 + 12 * m * d_rnn * 4,
    )
    out_shape = [jax.ShapeDtypeStruct((bsz, seq, d), F32), jax.ShapeDtypeStruct((bsz, 1, d_rnn), F32),
                 jax.ShapeDtypeStruct((bsz, V7X_SUBLANES, d_rnn), F32),
                 jax.ShapeDtypeStruct((bsz, keep, kv_w), F32), jax.ShapeDtypeStruct((bsz, keep, kv_w), F32)]
    return pl.pallas_call(
        functools.partial(_mix_body, nb=nb, tt=tt, qc=qc, hist=hist, keep=keep, mask_hist=mask_hist,
                          d_rnn=d_rnn, att_w=att_w, kv_w=kv_w),
        grid=(bsz // nb, n_t),
        in_specs=[pl.BlockSpec(memory_space=pltpu.SMEM),
                  bs(tt, d, follow_t=True), bs(CONV_WIDTH - 1, d_rnn), bs(1, d_rnn), bs(hist, kv_w),
                  bs(hist, kv_w)]
        + [_const_spec(w.shape) for w in (nw, win, cw, cb, wgate, bgate, lam, wbr, wba, wout)],
        out_specs=[bs(tt, d, follow_t=True), bs(1, d_rnn), bs(V7X_SUBLANES, d_rnn), bs(keep, kv_w),
                   bs(keep, kv_w)],
        out_shape=out_shape,
        scratch_shapes=[pltpu.VMEM((nb, V7X_SUBLANES, d_rnn), F32),
                        pltpu.VMEM((nb, 1, d_rnn), F32),
                        pltpu.VMEM((nb, N_KV_HEADS, hist + tt, V7X_LANES), BF16),
                        pltpu.VMEM((nb, N_KV_HEADS, hist + tt, V7X_LANES), BF16)],
        compiler_params=pltpu.CompilerParams(dimension_semantics=("parallel", "arbitrary"),
                                             vmem_limit_bytes=limit),
        name="mix",
    )(sinks, h, conv0, lru0, k0, v0, nw, win, cw, cb, wgate, bgate, lam, wbr, wba, wout)


def _tiles(bsz, seq):
    if seq % 512 == 0:
        return 1024, 1, 512, CHUNK
    assert seq <= CHUNK and (bsz * seq) % V7X_SUBLANES == 0
    return bsz * seq, bsz, seq, seq


def _layer(x, conv0, lru0, k0, v0, p, *, mask_hist, final_norm):
    bsz, seq, d = x.shape
    rows = bsz * seq
    tm, nb, tt, qc = _tiles(bsz, seq)
    mix_weights = (p["norm_mix"], p["w_in"], p["conv_w"], p["conv_b"], p["w_gate"], p["b_gate"], p["lru_lambda"],
                   p["w_branch_r"], p["w_branch_a"], p["w_out"])
    h = _ff(x.reshape(rows, d), p["norm_ff1"], p["ff1_gate"], p["ff1_up"], p["ff1_down"], p["norm_final"],
            final_norm=False, tm=tm)
    h2, lru_last, x_tail, k_new, v_new = _mix(
        p["attn_sinks"], h.reshape(bsz, seq, d), conv0, lru0, k0, v0, *mix_weights,
        nb=nb, tt=tt, qc=qc, mask_hist=mask_hist)
    y = _ff(h2.reshape(rows, d), p["norm_ff2"], p["ff2_gate"], p["ff2_up"], p["ff2_down"], p["norm_final"],
            final_norm=final_norm, tm=tm)
    return y.reshape(bsz, seq, d), k_new, v_new, x_tail[:, -(CONV_WIDTH - 1):], lru_last[:, 0]


def _sample_keys_all_visible(s, cw):
    q_pos = PAST_LEN + np.arange(s)
    k_pos = np.concatenate([PAST_LEN - cw + np.arange(cw), q_pos])
    qch = (q_pos // CHUNK)[:, None]
    kch = (k_pos // CHUNK)[None, :]
    valid = (kch <= qch) & (qch - kch <= WINDOW // CHUNK) & (k_pos[None, :] >= 0)
    return bool(valid.all())


def kernel(x_prompt, x_sample, cache_k, cache_v, state_conv, state_lru, norm_ff1, ff1_gate, ff1_up, ff1_down, norm_mix, w_in, conv_w, conv_b, w_rg, b_rg, w_ig, b_ig, lru_lambda, attn_sinks, w_branch, w_out, norm_ff2, ff2_gate, ff2_up, ff2_down, norm_final):
    depth = w_in.shape[0]
    bp, sp, d = x_prompt.shape
    bs_, ss, _ = x_sample.shape
    d_rnn = conv_w.shape[2]
    cw_len = cache_k.shape[2]
    kv_w = N_KV_HEADS * HEAD_DIM
    assert WINDOW == 2 * CHUNK and cw_len == WINDOW and sp % CHUNK == 0
    assert _sample_keys_all_visible(ss, cw_len)

    hp, hs = x_prompt, x_sample
    outs = [[] for _ in range(8)]
    for l in range(depth):
        p = {
            "norm_ff1": norm_ff1[l][None], "norm_mix": norm_mix[l][None], "norm_ff2": norm_ff2[l][None],
            "norm_final": norm_final[None],
            "ff1_gate": ff1_gate[l].astype(BF16), "ff1_up": ff1_up[l].astype(BF16),
            "ff1_down": ff1_down[l].astype(BF16),
            "ff2_gate": ff2_gate[l].astype(BF16), "ff2_up": ff2_up[l].astype(BF16),
            "ff2_down": ff2_down[l].astype(BF16),
            "w_in": w_in[l].astype(BF16),
            "conv_w": conv_w[l], "conv_b": conv_b[l][None],
            "w_gate": jnp.concatenate([w_rg[l], w_ig[l]], axis=-1).astype(BF16),
            "b_gate": jnp.concatenate([b_rg[l], b_ig[l]], axis=-1)[:, None, :],
            "lru_lambda": lru_lambda[l][None],
            "attn_sinks": attn_sinks[l],
            "w_branch_r": w_branch[l][:d_rnn].astype(BF16), "w_branch_a": w_branch[l][d_rnn:].astype(BF16),
            "w_out": w_out[l].astype(BF16),
        }
        last = l == depth - 1
        zeros = functools.partial(jnp.zeros, dtype=F32)
        hp, k_p, v_p, conv_p, lru_p = _layer(
            hp, zeros((bp, CONV_WIDTH - 1, d_rnn)), zeros((bp, 1, d_rnn)),
            zeros((bp, WINDOW, kv_w)), zeros((bp, WINDOW, kv_w)), p, mask_hist=True, final_norm=last)
        hs, k_s, v_s, conv_s, lru_s = _layer(
            hs, state_conv[l], state_lru[l][:, None, :],
            cache_k[l].reshape(bs_, cw_len, kv_w), cache_v[l].reshape(bs_, cw_len, kv_w), p,
            mask_hist=False, final_norm=last)
        per_layer = (
            k_p.reshape(bp, WINDOW, N_KV_HEADS, HEAD_DIM), v_p.reshape(bp, WINDOW, N_KV_HEADS, HEAD_DIM),
            conv_p, lru_p,
            k_s.reshape(bs_, ss, N_KV_HEADS, HEAD_DIM), v_s.reshape(bs_, ss, N_KV_HEADS, HEAD_DIM),
            conv_s, lru_s,
        )
        for dst, val in zip(outs, per_layer):
            dst.append(val)
    return (hp, hs) + tuple(jnp.stack(o) for o in outs)
```

```python
import functools

import numpy as np
import jax
import jax.numpy as jnp
from jax import lax
from jax.experimental import pallas as pl
from jax.experimental.pallas import tpu as pltpu

CHUNK = 64
N_HEADS = 16
N_KV_HEADS = 4
HEAD_DIM = 64
Q_GROUP = N_HEADS // N_KV_HEADS
WINDOW = 128
N_LRU_BLOCKS = 8
CONV_WIDTH = 4
LRU_C = 8.0
EPS = 1e-6
NEG_INF = -1e30
PAST_LEN = 1024
ATT_SCALE = HEAD_DIM ** -0.5

V7X_LANES = 128
V7X_SUBLANES = 8
V7X_MXU_DIM = 256
V7X_VMEM_BYTES = 64 * 1024 * 1024

F32 = jnp.float32
BF16 = jnp.bfloat16


def _dot(a, b):
    return jnp.dot(a, b, preferred_element_type=F32)


def _rms(x, w):
    return x * lax.rsqrt(jnp.mean(x * x, axis=-1, keepdims=True) + EPS) * w


def _sigmoid(x):
    return 0.5 * jnp.tanh(0.5 * x) + 0.5


def _sqrt_nonneg(z):
    return jnp.exp(0.5 * jnp.log(z))


def _const_spec(shape):
    zeros = (0,) * len(shape)
    return pl.BlockSpec(shape, lambda *_: zeros, pipeline_mode=pl.Buffered(1))


def _vmem_limit(block_bytes, resident_bytes, temp_bytes):
    need = 2 * block_bytes + resident_bytes + temp_bytes
    return int(min(V7X_VMEM_BYTES - (4 << 20), need))


def _ff_chunks(d_ff):
    step = 2 * V7X_MXU_DIM
    edges = list(range(0, d_ff, step)) + [d_ff]
    return tuple((a, b - a) for a, b in zip(edges[:-1], edges[1:]))


def _ff_body(x_ref, nw_ref, wg_ref, wu_ref, wd_ref, nf_ref, o_ref, *, chunks, final_norm, sub_rows):
    tm = x_ref.shape[0]

    def normed(r0):
        x = x_ref[r0:r0 + sub_rows, :]
        return x, _rms(x, nw_ref[...]).astype(BF16)

    pending = normed(0)
    for r0 in range(0, tm, sub_rows):
        x, xn = pending
        acc = None
        for ci, (c0, cs) in enumerate(chunks):
            gate = _dot(xn, wg_ref[:, c0:c0 + cs])
            up = _dot(xn, wu_ref[:, c0:c0 + cs])
            hid = (gate * jax.nn.sigmoid(gate) * up).astype(BF16)
            part = _dot(hid, wd_ref[c0:c0 + cs, :])
            acc = part if acc is None else acc + part
            if ci == 0 and r0 + sub_rows < tm:
                pending = normed(r0 + sub_rows)
        y = x + 0.5 * acc
        if final_norm:
            y = _rms(y, nf_ref[...])
        o_ref[r0:r0 + sub_rows, :] = y


def _ff(x, nw, wg, wu, wd, nf, *, final_norm, tm):
    rows, d = x.shape
    d_ff = wg.shape[1]
    assert rows % tm == 0
    sub_rows = min(tm, 2 * V7X_MXU_DIM)
    assert tm % sub_rows == 0
    chunks = _ff_chunks(d_ff)
    row_spec = pl.BlockSpec((tm, d), lambda i: (i, 0))
    widest = max(cs for _, cs in chunks)
    limit = _vmem_limit(
        block_bytes=2 * tm * d * 4,
        resident_bytes=3 * d * d_ff * 2 + 2 * 8 * d * 4,
        temp_bytes=2 * sub_rows * (d * (2 + 4 + 4 + 4) + widest * (4 + 4 + 4 + 2)),
    )
    return pl.pallas_call(
        functools.partial(_ff_body, chunks=chunks, final_norm=final_norm, sub_rows=sub_rows),
        grid=(rows // tm,),
        in_specs=[row_spec, _const_spec((1, d)), _const_spec((d, d_ff)), _const_spec((d, d_ff)),
                  _const_spec((d_ff, d)), _const_spec((1, d))],
        out_specs=row_spec,
        out_shape=jax.ShapeDtypeStruct((rows, d), F32),
        compiler_params=pltpu.CompilerParams(dimension_semantics=("parallel",), vmem_limit_bytes=limit),
        name="ff",
    )(x, nw, wg, wu, wd, nf)


def _store_head_pairs(buf, row0, x):
    n = x.shape[0]
    lane = lax.broadcasted_iota(jnp.int32, (n, V7X_LANES), 1)
    for kv in range(N_KV_HEADS):
        blk = x[:, V7X_LANES * (kv // 2):V7X_LANES * (kv // 2 + 1)]
        swapped = pltpu.roll(blk, HEAD_DIM, 1)
        keep = (lane < HEAD_DIM) if kv % 2 == 0 else (lane >= HEAD_DIM)
        buf[kv, row0:row0 + n, :] = jnp.where(keep, blk, swapped).astype(BF16)


def _causal_conv(x, tail8, cw, cb):
    sub8 = lax.broadcasted_iota(jnp.int32, tail8.shape, 0)

    def delayed(k):
        full = pltpu.roll(x, k, 0)
        head = jnp.where(sub8 < k, pltpu.roll(tail8, k, 0), full[0:V7X_SUBLANES])
        return jnp.concatenate([head, full[V7X_SUBLANES:]], axis=0)

    out = cb + cw[0:1] * delayed(3)
    out = out + cw[1:2] * delayed(2)
    out = out + cw[2:3] * delayed(1)
    return out + cw[3:4] * x


def _lru_scan(a, b, h0):
    rows = a.shape[0]
    sub = lax.broadcasted_iota(jnp.int32, (V7X_SUBLANES, a.shape[1]), 0)
    out = []
    carry = h0
    for r0 in range(0, rows, V7X_SUBLANES):
        ag = a[r0:r0 + V7X_SUBLANES]
        bg = b[r0:r0 + V7X_SUBLANES]
        shift = 1
        while shift < V7X_SUBLANES:
            take = sub >= shift
            a_prev = pltpu.roll(ag, shift, 0)
            b_prev = pltpu.roll(bg, shift, 0)
            bg = jnp.where(take, ag * b_prev + bg, bg)
            ag = jnp.where(take, ag * a_prev, ag)
            shift *= 2
        hg = ag * carry + bg
        out.append(hg)
        carry = hg[V7X_SUBLANES - 1:V7X_SUBLANES]
    return jnp.concatenate(out, axis=0)


def _attend_group(lhs, keys, vals, sinks, first_valid, filler):
    nt = (((1,), (1,)), ((), ()))
    tn = (((0,), (0,)), ((), ()))
    s = [lax.dot_general(k, q, nt, preferred_element_type=F32) for k, q in zip(keys, lhs)]
    if filler is not None:
        filler()
    if first_valid is not None:
        row = lax.broadcasted_iota(jnp.int32, s[0].shape, 0)
        s = [jnp.where(row >= first_valid, x, NEG_INF) for x in s]
    m = [jnp.maximum(jnp.max(x, axis=0, keepdims=True), sk) for x, sk in zip(s, sinks)]
    p = [jnp.exp(x - mx) for x, mx in zip(s, m)]
    denom = [jnp.sum(x, axis=0, keepdims=True) + jnp.exp(sk - mx) for x, sk, mx in zip(p, sinks, m)]
    pn = [(x * (1.0 / dn)).astype(BF16) for x, dn in zip(p, denom)]
    out = [lax.dot_general(x, v, tn, preferred_element_type=F32) for x, v in zip(pn, vals)]
    return out


def _conv_and_gate_logits(xr, cw_ref, cb_ref, wgate_ref, bgate_ref, tailbuf, xtail_ref, *, nb, tt, d_rnn):
    cw = cw_ref[...]
    cb = cb_ref[...]
    xc_rows = []
    for b in range(nb):
        xrb = xr[b * tt:(b + 1) * tt]
        xc_rows.append(_causal_conv(xrb, tailbuf[b], cw, cb))
        tailbuf[b] = xrb[tt - V7X_SUBLANES:tt]
        xtail_ref[b] = xrb[tt - V7X_SUBLANES:tt]
    xc = jnp.concatenate(xc_rows, axis=0)
    xcb = xc.astype(BF16)
    blk = d_rnn // N_LRU_BLOCKS
    r_parts, i_parts = [], []
    for n in range(N_LRU_BLOCKS):
        z = _dot(xcb[:, n * blk:(n + 1) * blk], wgate_ref[n]) + bgate_ref[n]
        r_parts.append(z[:, :blk])
        i_parts.append(z[:, blk:])
    return xc, jnp.concatenate(r_parts, axis=1), jnp.concatenate(i_parts, axis=1)


def _lru_coefficients(xc, r_pre, i_pre, lam_ref):
    neg_lam = -lam_ref[...]
    softplus = jnp.maximum(neg_lam, 0.0) + jnp.log1p(jnp.exp(-jnp.abs(neg_lam)))
    half_c = (0.5 * LRU_C) * softplus
    neg_log_a = half_c * jnp.tanh(0.5 * r_pre) + half_c
    a = jnp.exp(-neg_log_a)
    return a, _sqrt_nonneg(jnp.tanh(neg_log_a) * (a * a + 1.0)) * (_sigmoid(i_pre) * xc)


def _lru_states(a, bterm, hstate, lru_ref, *, nb, tt):
    h_rows = []
    for b in range(nb):
        hb = _lru_scan(a[b * tt:(b + 1) * tt], bterm[b * tt:(b + 1) * tt], hstate[b])
        hstate[b] = hb[tt - 1:tt]
        lru_ref[b] = hb[tt - 1:tt]
        h_rows.append(hb)
    return jnp.concatenate(h_rows, axis=0)


def _attention(u, win_ref, sinks_ref, kbuf, vbuf, knew_ref, vnew_ref, t, fillers, *, nb, tt, qc, hist, keep,
               mask_hist, c_q, c_kv, c_gr, kv_w):
    q = (_dot(u, win_ref[:, c_q:c_kv]) * ATT_SCALE).astype(BF16)
    kv_new = _dot(u, win_ref[:, c_kv:c_gr])
    k_new = kv_new[:, :kv_w]
    v_new = kv_new[:, kv_w:]
    lane_row = lax.broadcasted_iota(jnp.int32, (1, V7X_LANES), 1)
    sel_lo = jnp.where(lane_row < HEAD_DIM, 1.0, 0.0).astype(BF16)
    sel_hi = jnp.where(lane_row >= HEAD_DIM, 1.0, 0.0).astype(BF16)
    lo = lax.broadcasted_iota(jnp.int32, (qc, V7X_LANES), 1) < HEAD_DIM
    sinks = [sinks_ref[kv:kv + 1, :] for kv in range(N_KV_HEADS)]
    nkeys = hist + qc
    fillers = list(fillers)
    att_rows = []
    for b in range(nb):
        kb = kbuf.at[b]
        vb = vbuf.at[b]
        _store_head_pairs(kb, hist, k_new[b * tt:(b + 1) * tt])
        _store_head_pairs(vb, hist, v_new[b * tt:(b + 1) * tt])
        knew_ref[b] = k_new[(b + 1) * tt - keep:(b + 1) * tt]
        vnew_ref[b] = v_new[(b + 1) * tt - keep:(b + 1) * tt]
        for j in range(tt // qc):
            q0 = j * qc
            first_valid = jnp.where(t == 0, hist - q0, 0) if (mask_hist and q0 < hist) else None
            lhs = []
            for kv in range(N_KV_HEADS):
                c0 = kv * Q_GROUP * HEAD_DIM
                qa = q[b * tt + q0:b * tt + q0 + qc, c0:c0 + V7X_LANES]
                qb = q[b * tt + q0:b * tt + q0 + qc, c0 + V7X_LANES:c0 + 2 * V7X_LANES]
                lhs.append(jnp.concatenate([qa * sel_lo, qa * sel_hi, qb * sel_lo, qb * sel_hi], axis=0))
            outs = _attend_group(
                lhs, [kb[kv, q0:q0 + nkeys, :] for kv in range(N_KV_HEADS)],
                [vb[kv, q0:q0 + nkeys, :] for kv in range(N_KV_HEADS)], sinks, first_valid,
                fillers.pop(0) if fillers else None)
            heads = []
            for o in outs:
                heads.append(jnp.where(lo, o[0:qc], o[qc:2 * qc]))
                heads.append(jnp.where(lo, o[2 * qc:3 * qc], o[3 * qc:4 * qc]))
            att_rows.append(jnp.concatenate(heads, axis=1))
        if tt >= hist:
            for kv in range(N_KV_HEADS):
                kb[kv, 0:hist, :] = kb[kv, tt:tt + hist, :]
                vb[kv, 0:hist, :] = vb[kv, tt:tt + hist, :]
    for f in fillers:
        f()
    return jnp.concatenate(att_rows, axis=0).astype(BF16)


def _mix_body(sinks_ref, h_ref, conv0_ref, lru0_ref, k0_ref, v0_ref,
              nw_ref, win_ref, cw_ref, cb_ref, wgate_ref, bgate_ref, lam_ref, wbr_ref, wba_ref, wout_ref,
              h2_ref, lru_ref, xtail_ref, knew_ref, vnew_ref,
              tailbuf, hstate, kbuf, vbuf, *, nb, tt, qc, hist, keep, mask_hist, d_rnn, att_w, kv_w):
    t = pl.program_id(1)
    d = h_ref.shape[-1]
    m = nb * tt

    @pl.when(t == 0)
    def _load_state():
        tailbuf[...] = jnp.zeros_like(tailbuf)
        tailbuf[:, V7X_SUBLANES - (CONV_WIDTH - 1):V7X_SUBLANES, :] = conv0_ref[...]
        hstate[...] = lru0_ref[...]
        for b in range(nb):
            _store_head_pairs(kbuf.at[b], 0, k0_ref[b])
            _store_head_pairs(vbuf.at[b], 0, v0_ref[b])

    hin = h_ref[...].reshape(m, d)
    u = _rms(hin, nw_ref[...]).astype(BF16)
    c_g = d_rnn
    c_q = c_g + d_rnn
    c_kv = c_q + att_w
    c_gr = c_kv + 2 * kv_w
    c_ga = c_gr + d
    xr = _dot(u, win_ref[:, 0:c_g])
    st = {}

    def stage_conv():
        st["xc"], st["r_pre"], st["i_pre"] = _conv_and_gate_logits(
            xr, cw_ref, cb_ref, wgate_ref, bgate_ref, tailbuf, xtail_ref, nb=nb, tt=tt, d_rnn=d_rnn)
        st["g_rnn"] = _dot(u, win_ref[:, c_g:c_q])

    def stage_coefficients():
        st["a"], st["b"] = _lru_coefficients(st["xc"], st["r_pre"], st["i_pre"], lam_ref)
        st["gate_r"] = _sigmoid(_dot(u, win_ref[:, c_gr:c_ga]))

    def stage_scan():
        st["h"] = _lru_states(st["a"], st["b"], hstate, lru_ref, nb=nb, tt=tt)
        st["gate_a"] = _sigmoid(_dot(u, win_ref[:, c_ga:c_ga + d]))

    def stage_branch():
        rec = (jax.nn.gelu(st["g_rnn"]) * st["h"]).astype(BF16)
        st["gated_r"] = st["gate_r"] * _dot(rec, wbr_ref[...])

    att = _attention(u, win_ref, sinks_ref, kbuf, vbuf, knew_ref, vnew_ref, t,
                     [stage_conv, stage_coefficients, stage_scan, stage_branch], nb=nb, tt=tt, qc=qc,
                     hist=hist, keep=keep, mask_hist=mask_hist, c_q=c_q, c_kv=c_kv, c_gr=c_gr, kv_w=kv_w)
    merged = st["gated_r"] + st["gate_a"] * _dot(att, wba_ref[...])
    h2_ref[...] = (hin + _dot(merged.astype(BF16), wout_ref[...])).reshape(nb, tt, d)


def _mix(sinks, h, conv0, lru0, k0, v0, nw, win, cw, cb, wgate, bgate, lam, wbr, wba, wout,
         *, nb, tt, qc, mask_hist):
    bsz, seq, d = h.shape
    d_rnn = cw.shape[1]
    att_w = N_HEADS * HEAD_DIM
    kv_w = N_KV_HEADS * HEAD_DIM
    hist = k0.shape[1]
    keep = min(tt, hist)
    n_t = seq // tt
    assert seq % tt == 0 and bsz % nb == 0 and tt % qc == 0 and qc % 16 == 0
    assert n_t == 1 or tt >= hist
    assert 2 * HEAD_DIM == V7X_LANES and win.shape[1] == 2 * d_rnn + att_w + 2 * kv_w + 2 * d

    def bs(r, w, follow_t=False):
        return pl.BlockSpec((nb, r, w), (lambda b, t: (b, t, 0)) if follow_t else (lambda b, t: (b, 0, 0)))

    sink_rows = jnp.repeat(sinks.reshape(N_KV_HEADS, Q_GROUP), qc, axis=1)
    m = nb * tt
    weight_bytes = (win.size + wgate.size + wbr.size + wba.size + wout.size) * 2
    limit = _vmem_limit(
        block_bytes=2 * m * d * 4 + nb * (8 * d_rnn + 4 * hist * kv_w) * 4,
        resident_bytes=weight_bytes + 2 * nb * N_KV_HEADS * (hist + tt) * V7X_LANES * 2 + nb * 16 * d_rnn * 4,
        temp_bytes=m * win.shape[1] * 4 + 12 * m * d_rnn * 4,
    )
    out_shape = [jax.ShapeDtypeStruct((bsz, seq, d), F32), jax.ShapeDtypeStruct((bsz, 1, d_rnn), F32),
                 jax.ShapeDtypeStruct((bsz, V7X_SUBLANES, d_rnn), F32),
                 jax.ShapeDtypeStruct((bsz, keep, kv_w), F32), jax.ShapeDtypeStruct((bsz, keep, kv_w), F32)]
    return pl.pallas_call(
        functools.partial(_mix_body, nb=nb, tt=tt, qc=qc, hist=hist, keep=keep, mask_hist=mask_hist,
                          d_rnn=d_rnn, att_w=att_w, kv_w=kv_w),
        grid=(bsz // nb, n_t),
        in_specs=[_const_spec(sink_rows.shape),
                  bs(tt, d, follow_t=True), bs(CONV_WIDTH - 1, d_rnn), bs(1, d_rnn), bs(hist, kv_w),
                  bs(hist, kv_w)]
        + [_const_spec(w.shape) for w in (nw, win, cw, cb, wgate, bgate, lam, wbr, wba, wout)],
        out_specs=[bs(tt, d, follow_t=True), bs(1, d_rnn), bs(V7X_SUBLANES, d_rnn), bs(keep, kv_w),
                   bs(keep, kv_w)],
        out_shape=out_shape,
        scratch_shapes=[pltpu.VMEM((nb, V7X_SUBLANES, d_rnn), F32),
                        pltpu.VMEM((nb, 1, d_rnn), F32),
                        pltpu.VMEM((nb, N_KV_HEADS, hist + tt, V7X_LANES), BF16),
                        pltpu.VMEM((nb, N_KV_HEADS, hist + tt, V7X_LANES), BF16)],
        compiler_params=pltpu.CompilerParams(dimension_semantics=("parallel", "arbitrary"),
                                             vmem_limit_bytes=limit),
        name="mix",
    )(sink_rows, h, conv0, lru0, k0, v0, nw, win, cw, cb, wgate, bgate, lam, wbr, wba, wout)


def _tiles(bsz, seq):
    if seq % 512 == 0:
        return 1024, 1, 512, CHUNK
    assert seq <= CHUNK and (bsz * seq) % V7X_SUBLANES == 0
    return bsz * seq, bsz, seq, seq


def _layer(x, conv0, lru0, k0, v0, p, *, mask_hist, final_norm):
    bsz, seq, d = x.shape
    rows = bsz * seq
    tm, nb, tt, qc = _tiles(bsz, seq)
    mix_weights = (p["norm_mix"], p["w_in"], p["conv_w"], p["conv_b"], p["w_gate"], p["b_gate"], p["lru_lambda"],
                   p["w_branch_r"], p["w_branch_a"], p["w_out"])
    h = _ff(x.reshape(rows, d), p["norm_ff1"], p["ff1_gate"], p["ff1_up"], p["ff1_down"], p["norm_final"],
            final_norm=False, tm=tm)
    h2, lru_last, x_tail, k_new, v_new = _mix(
        p["attn_sinks"], h.reshape(bsz, seq, d), conv0, lru0, k0, v0, *mix_weights,
        nb=nb, tt=tt, qc=qc, mask_hist=mask_hist)
    y = _ff(h2.reshape(rows, d), p["norm_ff2"], p["ff2_gate"], p["ff2_up"], p["ff2_down"], p["norm_final"],
            final_norm=final_norm, tm=tm)
    return y.reshape(bsz, seq, d), k_new, v_new, x_tail[:, -(CONV_WIDTH - 1):], lru_last[:, 0]


def _sample_keys_all_visible(s, cw):
    q_pos = PAST_LEN + np.arange(s)
    k_pos = np.concatenate([PAST_LEN - cw + np.arange(cw), q_pos])
    qch = (q_pos // CHUNK)[:, None]
    kch = (k_pos // CHUNK)[None, :]
    valid = (kch <= qch) & (qch - kch <= WINDOW // CHUNK) & (k_pos[None, :] >= 0)
    return bool(valid.all())


def kernel(x_prompt, x_sample, cache_k, cache_v, state_conv, state_lru, norm_ff1, ff1_gate, ff1_up, ff1_down, norm_mix, w_in, conv_w, conv_b, w_rg, b_rg, w_ig, b_ig, lru_lambda, attn_sinks, w_branch, w_out, norm_ff2, ff2_gate, ff2_up, ff2_down, norm_final):
    depth = w_in.shape[0]
    bp, sp, d = x_prompt.shape
    bs_, ss, _ = x_sample.shape
    d_rnn = conv_w.shape[2]
    cw_len = cache_k.shape[2]
    kv_w = N_KV_HEADS * HEAD_DIM
    assert WINDOW == 2 * CHUNK and cw_len == WINDOW and sp % CHUNK == 0
    assert _sample_keys_all_visible(ss, cw_len)

    hp, hs = x_prompt, x_sample
    outs = [[] for _ in range(8)]
    for l in range(depth):
        p = {
            "norm_ff1": norm_ff1[l][None], "norm_mix": norm_mix[l][None], "norm_ff2": norm_ff2[l][None],
            "norm_final": norm_final[None],
            "ff1_gate": ff1_gate[l].astype(BF16), "ff1_up": ff1_up[l].astype(BF16),
            "ff1_down": ff1_down[l].astype(BF16),
            "ff2_gate": ff2_gate[l].astype(BF16), "ff2_up": ff2_up[l].astype(BF16),
            "ff2_down": ff2_down[l].astype(BF16),
            "w_in": w_in[l].astype(BF16),
            "conv_w": conv_w[l], "conv_b": conv_b[l][None],
            "w_gate": jnp.concatenate([w_rg[l], w_ig[l]], axis=-1).astype(BF16),
            "b_gate": jnp.concatenate([b_rg[l], b_ig[l]], axis=-1)[:, None, :],
            "lru_lambda": lru_lambda[l][None],
            "attn_sinks": attn_sinks[l],
            "w_branch_r": w_branch[l][:d_rnn].astype(BF16), "w_branch_a": w_branch[l][d_rnn:].astype(BF16),
            "w_out": w_out[l].astype(BF16),
        }
        last = l == depth - 1
        zeros = functools.partial(jnp.zeros, dtype=F32)
        hp, k_p, v_p, conv_p, lru_p = _layer(
            hp, zeros((bp, CONV_WIDTH - 1, d_rnn)), zeros((bp, 1, d_rnn)),
            zeros((bp, WINDOW, kv_w)), zeros((bp, WINDOW, kv_w)), p, mask_hist=True, final_norm=last)
        hs, k_s, v_s, conv_s, lru_s = _layer(
            hs, state_conv[l], state_lru[l][:, None, :],
            cache_k[l].reshape(bs_, cw_len, kv_w), cache_v[l].reshape(bs_, cw_len, kv_w), p,
            mask_hist=False, final_norm=last)
        per_layer = (
            k_p.reshape(bp, WINDOW, N_KV_HEADS, HEAD_DIM), v_p.reshape(bp, WINDOW, N_KV_HEADS, HEAD_DIM),
            conv_p, lru_p,
            k_s.reshape(bs_, ss, N_KV_HEADS, HEAD_DIM), v_s.reshape(bs_, ss, N_KV_HEADS, HEAD_DIM),
            conv_s, lru_s,
        )
        for dst, val in zip(outs, per_layer):
            dst.append(val)
    return (hp, hs) + tuple(jnp.stack(o) for o in outs)
```

```python
import functools

import numpy as np
import jax
import jax.numpy as jnp
from jax import lax
from jax.experimental import pallas as pl
from jax.experimental.pallas import tpu as pltpu

CHUNK = 64
N_HEADS = 16
N_KV_HEADS = 4
HEAD_DIM = 64
Q_GROUP = N_HEADS // N_KV_HEADS
WINDOW = 128
N_LRU_BLOCKS = 8
CONV_WIDTH = 4
LRU_C = 8.0
EPS = 1e-6
NEG_INF = -1e30
PAST_LEN = 1024
ATT_SCALE = HEAD_DIM ** -0.5

V7X_LANES = 128
V7X_SUBLANES = 8
V7X_MXU_DIM = 256
V7X_VMEM_BYTES = 64 * 1024 * 1024

F32 = jnp.float32
BF16 = jnp.bfloat16


def _dot(a, b):
    return jnp.dot(a, b, preferred_element_type=F32)


def _rms(x, w):
    return x * lax.rsqrt(jnp.mean(x * x, axis=-1, keepdims=True) + EPS) * w


def _sigmoid(x):
    return 0.5 * jnp.tanh(0.5 * x) + 0.5


def _sqrt_nonneg(z):
    return jnp.exp(0.5 * jnp.log(z))


def _const_spec(shape):
    zeros = (0,) * len(shape)
    return pl.BlockSpec(shape, lambda *_: zeros, pipeline_mode=pl.Buffered(1))


def _vmem_limit(block_bytes, resident_bytes, temp_bytes):
    need = 2 * block_bytes + resident_bytes + temp_bytes
    return int(min(V7X_VMEM_BYTES - (4 << 20), need))


def _ff_chunks(d_ff):
    step = 2 * V7X_MXU_DIM
    edges = list(range(0, d_ff, step)) + [d_ff]
    return tuple((a, b - a) for a, b in zip(edges[:-1], edges[1:]))


def _ff_body(x_ref, pre_ref, wg_ref, wu_ref, wd_ref, post_ref, o_ref, u_ref=None, *, chunks, sub_rows, normed_in,
             final_norm):
    tm = x_ref.shape[0]

    def normed(r0):
        x = x_ref[r0:r0 + sub_rows, :]
        if normed_in:
            return x, pre_ref[r0:r0 + sub_rows, :]
        return x, _rms(x, pre_ref[...]).astype(BF16)

    pending = normed(0)
    for r0 in range(0, tm, sub_rows):
        x, xn = pending
        acc = None
        for ci, (c0, cs) in enumerate(chunks):
            gate = _dot(xn, wg_ref[:, c0:c0 + cs])
            up = _dot(xn, wu_ref[:, c0:c0 + cs])
            hid = (gate * jax.nn.sigmoid(gate) * up).astype(BF16)
            part = _dot(hid, wd_ref[c0:c0 + cs, :])
            acc = part if acc is None else acc + part
            if ci == 0 and r0 + sub_rows < tm:
                pending = normed(r0 + sub_rows)
        y = x + 0.5 * acc
        if final_norm:
            y = _rms(y, post_ref[...])
        o_ref[r0:r0 + sub_rows, :] = y
        if u_ref is not None:
            u_ref[r0:r0 + sub_rows, :] = _rms(y, post_ref[...]).astype(BF16)


def _ff(x, pre, wg, wu, wd, post, *, normed_in, final_norm, normed_out, tm):
    rows, d = x.shape
    d_ff = wg.shape[1]
    assert rows % tm == 0 and not (final_norm and normed_out)
    sub_rows = min(tm, 2 * V7X_MXU_DIM)
    assert tm % sub_rows == 0
    chunks = _ff_chunks(d_ff)
    row_spec = pl.BlockSpec((tm, d), lambda i: (i, 0))
    widest = max(cs for _, cs in chunks)
    limit = _vmem_limit(
        block_bytes=tm * d * (4 + 4 + 2),
        resident_bytes=3 * d * d_ff * 2 + 2 * 8 * d * 4,
        temp_bytes=2 * sub_rows * (d * (2 + 4 + 4 + 4) + widest * (4 + 4 + 4 + 2)),
    )
    out_shape = [jax.ShapeDtypeStruct((rows, d), F32)] + [jax.ShapeDtypeStruct((rows, d), BF16)] * normed_out
    outs = pl.pallas_call(
        functools.partial(_ff_body, chunks=chunks, sub_rows=sub_rows, normed_in=normed_in, final_norm=final_norm),
        grid=(rows // tm,),
        in_specs=[row_spec, row_spec if normed_in else _const_spec((1, d)), _const_spec((d, d_ff)),
                  _const_spec((d, d_ff)), _const_spec((d_ff, d)), _const_spec((1, d))],
        out_specs=[row_spec] * len(out_shape),
        out_shape=out_shape,
        compiler_params=pltpu.CompilerParams(dimension_semantics=("parallel",), vmem_limit_bytes=limit),
        name="ff",
    )(x, pre, wg, wu, wd, post)
    return outs if normed_out else outs[0]


def _store_head_pairs(buf, row0, x):
    n = x.shape[0]
    lane = lax.broadcasted_iota(jnp.int32, (n, V7X_LANES), 1)
    for kv in range(N_KV_HEADS):
        blk = x[:, V7X_LANES * (kv // 2):V7X_LANES * (kv // 2 + 1)]
        swapped = pltpu.roll(blk, HEAD_DIM, 1)
        keep = (lane < HEAD_DIM) if kv % 2 == 0 else (lane >= HEAD_DIM)
        buf[kv, row0:row0 + n, :] = jnp.where(keep, blk, swapped).astype(BF16)


def _causal_conv(x, tail8, cw, cb):
    sub8 = lax.broadcasted_iota(jnp.int32, tail8.shape, 0)

    def delayed(k):
        full = pltpu.roll(x, k, 0)
        head = jnp.where(sub8 < k, pltpu.roll(tail8, k, 0), full[0:V7X_SUBLANES])
        return jnp.concatenate([head, full[V7X_SUBLANES:]], axis=0)

    out = cb + cw[0:1] * delayed(3)
    out = out + cw[1:2] * delayed(2)
    out = out + cw[2:3] * delayed(1)
    return out + cw[3:4] * x


def _lru_scan(a, b, h0):
    rows = a.shape[0]
    sub = lax.broadcasted_iota(jnp.int32, (V7X_SUBLANES, a.shape[1]), 0)
    out = []
    carry = h0
    for r0 in range(0, rows, V7X_SUBLANES):
        ag = a[r0:r0 + V7X_SUBLANES]
        bg = b[r0:r0 + V7X_SUBLANES]
        shift = 1
        while shift < V7X_SUBLANES:
            take = sub >= shift
            a_prev = pltpu.roll(ag, shift, 0)
            b_prev = pltpu.roll(bg, shift, 0)
            bg = jnp.where(take, ag * b_prev + bg, bg)
            ag = jnp.where(take, ag * a_prev, ag)
            shift *= 2
        hg = ag * carry + bg
        out.append(hg)
        carry = hg[V7X_SUBLANES - 1:V7X_SUBLANES]
    return jnp.concatenate(out, axis=0)


def _attend_group(lhs, keys, vals, sinks, first_valid, filler):
    nt = (((1,), (1,)), ((), ()))
    tn = (((0,), (0,)), ((), ()))
    s = [lax.dot_general(k, q, nt, preferred_element_type=F32) for k, q in zip(keys, lhs)]
    if filler is not None:
        filler()
    if first_valid is not None:
        row = lax.broadcasted_iota(jnp.int32, s[0].shape, 0)
        s = [jnp.where(row >= first_valid, x, NEG_INF) for x in s]
    m = [jnp.maximum(jnp.max(x, axis=0, keepdims=True), sk) for x, sk in zip(s, sinks)]
    p = [jnp.exp(x - mx) for x, mx in zip(s, m)]
    denom = [jnp.sum(x, axis=0, keepdims=True) + jnp.exp(sk - mx) for x, sk, mx in zip(p, sinks, m)]
    pn = [(x * (1.0 / dn)).astype(BF16) for x, dn in zip(p, denom)]
    out = [lax.dot_general(x, v, tn, preferred_element_type=F32) for x, v in zip(pn, vals)]
    return out


def _conv_and_gate_logits(xr, cw_ref, cb_ref, wgate_ref, bgate_ref, tailbuf, xtail_ref, *, nb, tt, d_rnn):
    cw = cw_ref[...]
    cb = cb_ref[...]
    xc_rows = []
    for b in range(nb):
        xrb = xr[b * tt:(b + 1) * tt]
        xc_rows.append(_causal_conv(xrb, tailbuf[b], cw, cb))
        tailbuf[b] = xrb[tt - V7X_SUBLANES:tt]
        xtail_ref[b] = xrb[tt - V7X_SUBLANES:tt]
    xc = jnp.concatenate(xc_rows, axis=0)
    xcb = xc.astype(BF16)
    blk = d_rnn // N_LRU_BLOCKS
    r_parts, i_parts = [], []
    for n in range(N_LRU_BLOCKS):
        z = _dot(xcb[:, n * blk:(n + 1) * blk], wgate_ref[n]) + bgate_ref[n]
        r_parts.append(z[:, :blk])
        i_parts.append(z[:, blk:])
    return xc, jnp.concatenate(r_parts, axis=1), jnp.concatenate(i_parts, axis=1)


def _lru_coefficients(xc, r_pre, i_pre, lam_ref):
    neg_lam = -lam_ref[...]
    softplus = jnp.maximum(neg_lam, 0.0) + jnp.log1p(jnp.exp(-jnp.abs(neg_lam)))
    half_c = (0.5 * LRU_C) * softplus
    neg_log_a = half_c * jnp.tanh(0.5 * r_pre) + half_c
    a = jnp.exp(-neg_log_a)
    return a, _sqrt_nonneg(jnp.tanh(neg_log_a) * (a * a + 1.0)) * (_sigmoid(i_pre) * xc)


def _lru_states(a, bterm, hstate, lru_ref, *, nb, tt):
    h_rows = []
    for b in range(nb):
        hb = _lru_scan(a[b * tt:(b + 1) * tt], bterm[b * tt:(b + 1) * tt], hstate[b])
        hstate[b] = hb[tt - 1:tt]
        lru_ref[b] = hb[tt - 1:tt]
        h_rows.append(hb)
    return jnp.concatenate(h_rows, axis=0)


def _attention(u, win_ref, sinks_ref, kbuf, vbuf, knew_ref, vnew_ref, t, fillers, *, nb, tt, qc, hist, keep,
               mask_hist, c_q, c_kv, c_gr, kv_w):
    q = (_dot(u, win_ref[:, c_q:c_kv]) * ATT_SCALE).astype(BF16)
    kv_new = _dot(u, win_ref[:, c_kv:c_gr])
    k_new = kv_new[:, :kv_w]
    v_new = kv_new[:, kv_w:]
    lane_row = lax.broadcasted_iota(jnp.int32, (1, V7X_LANES), 1)
    sel_lo = jnp.where(lane_row < HEAD_DIM, 1.0, 0.0).astype(BF16)
    sel_hi = jnp.where(lane_row >= HEAD_DIM, 1.0, 0.0).astype(BF16)
    lo = lax.broadcasted_iota(jnp.int32, (qc, V7X_LANES), 1) < HEAD_DIM
    sinks = [sinks_ref[kv:kv + 1, :] for kv in range(N_KV_HEADS)]
    nkeys = hist + qc
    fillers = list(fillers)
    att_rows = []
    for b in range(nb):
        kb = kbuf.at[b]
        vb = vbuf.at[b]
        _store_head_pairs(kb, hist, k_new[b * tt:(b + 1) * tt])
        _store_head_pairs(vb, hist, v_new[b * tt:(b + 1) * tt])
        knew_ref[b] = k_new[(b + 1) * tt - keep:(b + 1) * tt]
        vnew_ref[b] = v_new[(b + 1) * tt - keep:(b + 1) * tt]
        for j in range(tt // qc):
            q0 = j * qc
            first_valid = jnp.where(t == 0, hist - q0, 0) if (mask_hist and q0 < hist) else None
            lhs = []
            for kv in range(N_KV_HEADS):
                c0 = kv * Q_GROUP * HEAD_DIM
                qa = q[b * tt + q0:b * tt + q0 + qc, c0:c0 + V7X_LANES]
                qb = q[b * tt + q0:b * tt + q0 + qc, c0 + V7X_LANES:c0 + 2 * V7X_LANES]
                lhs.append(jnp.concatenate([qa * sel_lo, qa * sel_hi, qb * sel_lo, qb * sel_hi], axis=0))
            outs = _attend_group(
                lhs, [kb[kv, q0:q0 + nkeys, :] for kv in range(N_KV_HEADS)],
                [vb[kv, q0:q0 + nkeys, :] for kv in range(N_KV_HEADS)], sinks, first_valid,
                fillers.pop(0) if fillers else None)
            heads = []
            for o in outs:
                heads.append(jnp.where(lo, o[0:qc], o[qc:2 * qc]))
                heads.append(jnp.where(lo, o[2 * qc:3 * qc], o[3 * qc:4 * qc]))
            att_rows.append(jnp.concatenate(heads, axis=1))
        if tt >= hist:
            for kv in range(N_KV_HEADS):
                kb[kv, 0:hist, :] = kb[kv, tt:tt + hist, :]
                vb[kv, 0:hist, :] = vb[kv, tt:tt + hist, :]
    for f in fillers:
        f()
    return jnp.concatenate(att_rows, axis=0).astype(BF16)


def _mix_body(sinks_ref, h_ref, u_ref, conv0_ref, lru0_ref, k0_ref, v0_ref,
              nnext_ref, win_ref, cw_ref, cb_ref, wgate_ref, bgate_ref, lam_ref, wbr_ref, wba_ref, wout_ref,
              h2_ref, unext_ref, lru_ref, xtail_ref, knew_ref, vnew_ref,
              tailbuf, hstate, kbuf, vbuf, *, nb, tt, qc, hist, keep, mask_hist, d_rnn, att_w, kv_w):
    t = pl.program_id(1)
    d = h_ref.shape[-1]
    m = nb * tt

    @pl.when(t == 0)
    def _load_state():
        tailbuf[...] = jnp.zeros_like(tailbuf)
        tailbuf[:, V7X_SUBLANES - (CONV_WIDTH - 1):V7X_SUBLANES, :] = conv0_ref[...]
        hstate[...] = lru0_ref[...]
        for b in range(nb):
            _store_head_pairs(kbuf.at[b], 0, k0_ref[b])
            _store_head_pairs(vbuf.at[b], 0, v0_ref[b])

    hin = h_ref[...].reshape(m, d)
    u = u_ref[...].reshape(m, d)
    c_g = d_rnn
    c_q = c_g + d_rnn
    c_kv = c_q + att_w
    c_gr = c_kv + 2 * kv_w
    c_ga = c_gr + d
    xr = _dot(u, win_ref[:, 0:c_g])
    st = {}

    def stage_conv():
        st["xc"], st["r_pre"], st["i_pre"] = _conv_and_gate_logits(
            xr, cw_ref, cb_ref, wgate_ref, bgate_ref, tailbuf, xtail_ref, nb=nb, tt=tt, d_rnn=d_rnn)
        st["g_rnn"] = _dot(u, win_ref[:, c_g:c_q])

    def stage_coefficients():
        st["a"], st["b"] = _lru_coefficients(st["xc"], st["r_pre"], st["i_pre"], lam_ref)
        st["gate_r"] = _sigmoid(_dot(u, win_ref[:, c_gr:c_ga]))

    def stage_scan():
        st["h"] = _lru_states(st["a"], st["b"], hstate, lru_ref, nb=nb, tt=tt)
        st["gate_a"] = _sigmoid(_dot(u, win_ref[:, c_ga:c_ga + d]))

    def stage_branch():
        rec = (jax.nn.gelu(st["g_rnn"]) * st["h"]).astype(BF16)
        st["gated_r"] = st["gate_r"] * _dot(rec, wbr_ref[...])

    att = _attention(u, win_ref, sinks_ref, kbuf, vbuf, knew_ref, vnew_ref, t,
                     [stage_conv, stage_coefficients, stage_scan, stage_branch], nb=nb, tt=tt, qc=qc,
                     hist=hist, keep=keep, mask_hist=mask_hist, c_q=c_q, c_kv=c_kv, c_gr=c_gr, kv_w=kv_w)
    merged = st["gated_r"] + st["gate_a"] * _dot(att, wba_ref[...])
    h2 = hin + _dot(merged.astype(BF16), wout_ref[...])
    h2_ref[...] = h2.reshape(nb, tt, d)
    unext_ref[...] = _rms(h2, nnext_ref[...]).astype(BF16).reshape(nb, tt, d)


def _mix(sinks, h, u, conv0, lru0, k0, v0, nnext, win, cw, cb, wgate, bgate, lam, wbr, wba, wout,
         *, nb, tt, qc, mask_hist):
    bsz, seq, d = h.shape
    d_rnn = cw.shape[1]
    att_w = N_HEADS * HEAD_DIM
    kv_w = N_KV_HEADS * HEAD_DIM
    hist = k0.shape[1]
    keep = min(tt, hist)
    n_t = seq // tt
    assert seq % tt == 0 and bsz % nb == 0 and tt % qc == 0 and qc % 16 == 0
    assert n_t == 1 or tt >= hist
    assert 2 * HEAD_DIM == V7X_LANES and win.shape[1] == 2 * d_rnn + att_w + 2 * kv_w + 2 * d

    def bs(r, w, follow_t=False):
        return pl.BlockSpec((nb, r, w), (lambda b, t: (b, t, 0)) if follow_t else (lambda b, t: (b, 0, 0)))

    sink_rows = jnp.repeat(sinks.reshape(N_KV_HEADS, Q_GROUP), qc, axis=1)
    m = nb * tt
    weight_bytes = (win.size + wgate.size + wbr.size + wba.size + wout.size) * 2
    limit = _vmem_limit(
        block_bytes=2 * m * d * (4 + 2) + nb * (8 * d_rnn + 4 * hist * kv_w) * 4,
        resident_bytes=weight_bytes + 2 * nb * N_KV_HEADS * (hist + tt) * V7X_LANES * 2 + nb * 16 * d_rnn * 4,
        temp_bytes=m * win.shape[1] * 4 + 12 * m * d_rnn * 4,
    )
    out_shape = [jax.ShapeDtypeStruct((bsz, seq, d), F32), jax.ShapeDtypeStruct((bsz, seq, d), BF16),
                 jax.ShapeDtypeStruct((bsz, 1, d_rnn), F32),
                 jax.ShapeDtypeStruct((bsz, V7X_SUBLANES, d_rnn), F32),
                 jax.ShapeDtypeStruct((bsz, keep, kv_w), F32), jax.ShapeDtypeStruct((bsz, keep, kv_w), F32)]
    return pl.pallas_call(
        functools.partial(_mix_body, nb=nb, tt=tt, qc=qc, hist=hist, keep=keep, mask_hist=mask_hist,
                          d_rnn=d_rnn, att_w=att_w, kv_w=kv_w),
        grid=(bsz // nb, n_t),
        in_specs=[_const_spec(sink_rows.shape),
                  bs(tt, d, follow_t=True), bs(tt, d, follow_t=True), bs(CONV_WIDTH - 1, d_rnn), bs(1, d_rnn),
                  bs(hist, kv_w), bs(hist, kv_w)]
        + [_const_spec(w.shape) for w in (nnext, win, cw, cb, wgate, bgate, lam, wbr, wba, wout)],
        out_specs=[bs(tt, d, follow_t=True), bs(tt, d, follow_t=True), bs(1, d_rnn), bs(V7X_SUBLANES, d_rnn),
                   bs(keep, kv_w), bs(keep, kv_w)],
        out_shape=out_shape,
        scratch_shapes=[pltpu.VMEM((nb, V7X_SUBLANES, d_rnn), F32),
                        pltpu.VMEM((nb, 1, d_rnn), F32),
                        pltpu.VMEM((nb, N_KV_HEADS, hist + tt, V7X_LANES), BF16),
                        pltpu.VMEM((nb, N_KV_HEADS, hist + tt, V7X_LANES), BF16)],
        compiler_params=pltpu.CompilerParams(dimension_semantics=("parallel", "arbitrary"),
                                             vmem_limit_bytes=limit),
        name="mix",
    )(sink_rows, h, u, conv0, lru0, k0, v0, nnext, win, cw, cb, wgate, bgate, lam, wbr, wba, wout)


def _tiles(bsz, seq):
    if seq % 512 == 0:
        return 1024, 1, 512, CHUNK
    assert seq <= CHUNK and (bsz * seq) % V7X_SUBLANES == 0
    return bsz * seq, bsz, seq, seq


def _layer(x, conv0, lru0, k0, v0, p, *, mask_hist, final_norm):
    bsz, seq, d = x.shape
    rows = bsz * seq
    tm, nb, tt, qc = _tiles(bsz, seq)
    mix_weights = (p["norm_ff2"], p["w_in"], p["conv_w"], p["conv_b"], p["w_gate"], p["b_gate"], p["lru_lambda"],
                   p["w_branch_r"], p["w_branch_a"], p["w_out"])
    h, u = _ff(x.reshape(rows, d), p["norm_ff1"], p["ff1_gate"], p["ff1_up"], p["ff1_down"], p["norm_mix"],
               normed_in=False, final_norm=False, normed_out=True, tm=tm)
    h2, u2, lru_last, x_tail, k_new, v_new = _mix(
        p["attn_sinks"], h.reshape(bsz, seq, d), u.reshape(bsz, seq, d), conv0, lru0, k0, v0, *mix_weights,
        nb=nb, tt=tt, qc=qc, mask_hist=mask_hist)
    y = _ff(h2.reshape(rows, d), u2.reshape(rows, d), p["ff2_gate"], p["ff2_up"], p["ff2_down"], p["norm_final"],
            normed_in=True, final_norm=final_norm, normed_out=False, tm=tm)
    return y.reshape(bsz, seq, d), k_new, v_new, x_tail[:, -(CONV_WIDTH - 1):], lru_last[:, 0]


def _sample_keys_all_visible(s, cw):
    q_pos = PAST_LEN + np.arange(s)
    k_pos = np.concatenate([PAST_LEN - cw + np.arange(cw), q_pos])
    qch = (q_pos // CHUNK)[:, None]
    kch = (k_pos // CHUNK)[None, :]
    valid = (kch <= qch) & (qch - kch <= WINDOW // CHUNK) & (k_pos[None, :] >= 0)
    return bool(valid.all())


def kernel(x_prompt, x_sample, cache_k, cache_v, state_conv, state_lru, norm_ff1, ff1_gate, ff1_up, ff1_down, norm_mix, w_in, conv_w, conv_b, w_rg, b_rg, w_ig, b_ig, lru_lambda, attn_sinks, w_branch, w_out, norm_ff2, ff2_gate, ff2_up, ff2_down, norm_final):
    depth = w_in.shape[0]
    bp, sp, d = x_prompt.shape
    bs_, ss, _ = x_sample.shape
    d_rnn = conv_w.shape[2]
    cw_len = cache_k.shape[2]
    kv_w = N_KV_HEADS * HEAD_DIM
    assert WINDOW == 2 * CHUNK and cw_len == WINDOW and sp % CHUNK == 0
    assert _sample_keys_all_visible(ss, cw_len)

    hp, hs = x_prompt, x_sample
    outs = [[] for _ in range(8)]
    for l in range(depth):
        p = {
            "norm_ff1": norm_ff1[l][None], "norm_mix": norm_mix[l][None], "norm_ff2": norm_ff2[l][None],
            "norm_final": norm_final[None],
            "ff1_gate": ff1_gate[l].astype(BF16), "ff1_up": ff1_up[l].astype(BF16),
            "ff1_down": ff1_down[l].astype(BF16),
            "ff2_gate": ff2_gate[l].astype(BF16), "ff2_up": ff2_up[l].astype(BF16),
            "ff2_down": ff2_down[l].astype(BF16),
            "w_in": w_in[l].astype(BF16),
            "conv_w": conv_w[l], "conv_b": conv_b[l][None],
            "w_gate": jnp.concatenate([w_rg[l], w_ig[l]], axis=-1).astype(BF16),
            "b_gate": jnp.concatenate([b_rg[l], b_ig[l]], axis=-1)[:, None, :],
            "lru_lambda": lru_lambda[l][None],
            "attn_sinks": attn_sinks[l],
            "w_branch_r": w_branch[l][:d_rnn].astype(BF16), "w_branch_a": w_branch[l][d_rnn:].astype(BF16),
            "w_out": w_out[l].astype(BF16),
        }
        last = l == depth - 1
        zeros = functools.partial(jnp.zeros, dtype=F32)
        hp, k_p, v_p, conv_p, lru_p = _layer(
            hp, zeros((bp, CONV_WIDTH - 1, d_rnn)), zeros((bp, 1, d_rnn)),
            zeros((bp, WINDOW, kv_w)), zeros((bp, WINDOW, kv_w)), p, mask_hist=True, final_norm=last)
        hs, k_s, v_s, conv_s, lru_s = _layer(
            hs, state_conv[l], state_lru[l][:, None, :],
            cache_k[l].reshape(bs_, cw_len, kv_w), cache_v[l].reshape(bs_, cw_len, kv_w), p,
            mask_hist=False, final_norm=last)
        per_layer = (
            k_p.reshape(bp, WINDOW, N_KV_HEADS, HEAD_DIM), v_p.reshape(bp, WINDOW, N_KV_HEADS, HEAD_DIM),
            conv_p, lru_p,
            k_s.reshape(bs_, ss, N_KV_HEADS, HEAD_DIM), v_s.reshape(bs_, ss, N_KV_HEADS, HEAD_DIM),
            conv_s, lru_s,
        )
        for dst, val in zip(outs, per_layer):
            dst.append(val)
    return (hp, hs) + tuple(jnp.stack(o) for o in outs)
```
